```python
import jax, jax.numpy as jnp
from jax import lax
import numpy as np

D_MODEL = 1024
BATCH = 32
SEQ = 2048
DEPTH = 1

CHUNK = 64
N_META = 16
EPS = 1e-6
POOL_WIDTH = 512
POOL_GROUPS = 4
POOL_GROUP_DIM = POOL_WIDTH // POOL_GROUPS
POOL_WINDOWS = (2, 4, 8, 16)
GLA_HEADS = 4
GLA_DK = 64
GLA_DV = 128
GLA_KEY = GLA_HEADS * GLA_DK
GLA_VAL = GLA_HEADS * GLA_DV
GLA_GATE_RANK = 16
GLA_TAU = 16.0
D_FF = 4 * D_MODEL
IN_SIZES = (POOL_WIDTH, GLA_KEY, GLA_KEY, GLA_VAL, GLA_VAL, GLA_GATE_RANK, D_MODEL, D_MODEL)
N_IN = sum(IN_SIZES)
IN_SPLITS = [int(s) for s in np.cumsum(IN_SIZES)[:-1]]

kernel_name = "hybrid_pool_gla_gated_block"


def rmsnorm(x, g):
    xf = x.astype(jnp.float32)
    y = xf * lax.rsqrt(jnp.mean(xf * xf, axis=-1, keepdims=True) + EPS)
    return (y * g.astype(jnp.float32)).astype(x.dtype)


def multiscale_pool(a):
    B, L, _ = a.shape
    af = a.astype(jnp.float32)
    cs = jnp.concatenate([jnp.zeros((B, 1, POOL_WIDTH), jnp.float32), jnp.cumsum(af, axis=1)], axis=1)
    t = jnp.arange(L)
    outs = []
    for g, w in enumerate(POOL_WINDOWS):
        lo_c, hi_c = g * POOL_GROUP_DIM, (g + 1) * POOL_GROUP_DIM
        hi = cs[:, 1:, lo_c:hi_c]
        lo = jnp.concatenate([jnp.zeros((B, w - 1, POOL_GROUP_DIM), jnp.float32), cs[:, :L - w + 1, lo_c:hi_c]], axis=1)
        cnt = jnp.minimum(t + 1, w).astype(jnp.float32)[None, :, None]
        outs.append((hi - lo) / cnt)
    return (jnp.concatenate(outs, axis=-1) - af).astype(a.dtype)


def gla_chunk_step(S, inp):
    qc, kc, vc, ac = inp
    b = jnp.cumsum(ac, axis=2)
    decay = jnp.exp(-jnp.abs(b[:, :, :, None, :] - b[:, :, None, :, :]))
    scores = jnp.einsum('bhtk,bhsk,bhtsk->bhts', qc, kc, decay)
    o = jnp.einsum('bhts,bhsv->bhtv', scores, vc) + jnp.einsum('bhtk,bhkv->bhtv', qc * jnp.exp(b), S)
    b_end = b[:, :, -1:, :]
    S_new = jnp.exp(b_end[:, :, 0, :])[..., None] * S + jnp.einsum('bhsk,bhsv->bhkv', kc * jnp.exp(b_end - b), vc)
    return S_new, o


def gla_chunked(q, k, v, log_a):
    B = q.shape[0]
    pad = CHUNK - N_META

    def prep(t):
        t = jnp.pad(t.astype(jnp.float32), ((0, 0), (pad, 0), (0, 0), (0, 0)))
        n = t.shape[1] // CHUNK
        return t.reshape(B, n, CHUNK, GLA_HEADS, t.shape[-1]).transpose(1, 0, 3, 2, 4)

    qc, kc, vc, ac = prep(q), prep(k), prep(v), prep(log_a)
    S0 = jnp.zeros((B, GLA_HEADS, GLA_DK, GLA_DV), jnp.float32)
    _, o = lax.scan(gla_chunk_step, S0, (qc, kc, vc, ac))
    o = o.transpose(1, 0, 3, 2, 4).reshape(B, -1, GLA_HEADS, GLA_DV)[:, pad:]
    return o.astype(v.dtype)


def hybrid_mixer(u, w_in, w_pool_mix, pool_scale, w_pool_out, w_gate_up, b_gate, gla_norm, w_gla_out, w_out):
    B, L, _ = u.shape
    z = jnp.einsum('bld,dn->bln', u, w_in)
    a, q, k, v, r, lr, ga, gb = jnp.split(z, IN_SPLITS, axis=-1)
    p = multiscale_pool(a).reshape(B, L, POOL_GROUPS, POOL_GROUP_DIM)
    p = jnp.einsum('blgc,gce->blge', p, w_pool_mix).reshape(B, L, POOL_WIDTH) * pool_scale
    y_a = jnp.einsum('blc,cd->bld', p, w_pool_out)
    log_a = jax.nn.log_sigmoid((lr @ w_gate_up + b_gate).astype(jnp.float32)) / GLA_TAU
    q = q.reshape(B, L, GLA_HEADS, GLA_DK) * (GLA_DK ** -0.5)
    k = k.reshape(B, L, GLA_HEADS, GLA_DK)
    v = v.reshape(B, L, GLA_HEADS, GLA_DV)
    log_a = log_a.reshape(B, L, GLA_HEADS, GLA_DK)
    o = gla_chunked(q, k, v, log_a)
    o = rmsnorm(o, gla_norm.reshape(GLA_HEADS, GLA_DV)) * jax.nn.silu(r).reshape(B, L, GLA_HEADS, GLA_DV)
    y_b = jnp.einsum('blc,cd->bld', o.reshape(B, L, GLA_VAL), w_gla_out)
    m = jax.nn.sigmoid(ga) * y_a + jax.nn.sigmoid(gb) * y_b
    return jnp.einsum('bld,de->ble', m, w_out)


def squared_relu_mlp(u, w_ff1, w_ff2):
    hdn = jnp.square(jax.nn.relu(jnp.einsum('bld,df->blf', u, w_ff1)))
    return jnp.einsum('blf,fd->bld', hdn, w_ff2)


def setup_inputs(seed: int = 0) -> dict:
    key = jax.random.key(seed)
    ks = jax.random.split(key, 20)
    f32 = jnp.float32
    nrm = lambda k, shape, s: jax.random.normal(k, shape, f32) * s
    return {
        "x": jax.random.normal(ks[0], (BATCH, SEQ, D_MODEL), f32),
        "meta_tokens": nrm(ks[1], (N_META, D_MODEL), 1.0),
        "norm_mix": 1.0 + nrm(ks[2], (DEPTH, D_MODEL), 0.05),
        "w_in": nrm(ks[3], (DEPTH, D_MODEL, N_IN), D_MODEL ** -0.5),
        "w_pool_mix": nrm(ks[4], (DEPTH, POOL_GROUPS, POOL_GROUP_DIM, POOL_GROUP_DIM), POOL_GROUP_DIM ** -0.5),
        "pool_scale": 1.0 + nrm(ks[5], (DEPTH, POOL_WIDTH), 0.1),
        "w_pool_out": nrm(ks[6], (DEPTH, POOL_WIDTH, D_MODEL), POOL_WIDTH ** -0.5),
        "w_gate_up": nrm(ks[7], (DEPTH, GLA_GATE_RANK, GLA_KEY), GLA_GATE_RANK ** -0.5),
        "b_gate": nrm(ks[8], (DEPTH, GLA_KEY), 0.1),
        "gla_norm": 1.0 + nrm(ks[9], (DEPTH, GLA_VAL), 0.05),
        "w_gla_out": nrm(ks[10], (DEPTH, GLA_VAL, D_MODEL), GLA_VAL ** -0.5),
        "w_out": nrm(ks[11], (DEPTH, D_MODEL, D_MODEL), D_MODEL ** -0.5),
        "norm_ffn": 1.0 + nrm(ks[12], (DEPTH, D_MODEL), 0.05),
        "w_ff1": nrm(ks[13], (DEPTH, D_MODEL, D_FF), D_MODEL ** -0.5),
        "w_ff2": nrm(ks[14], (DEPTH, D_FF, D_MODEL), D_FF ** -0.5),
        "norm_final": 1.0 + nrm(ks[15], (D_MODEL,), 0.05),
    }


def reference(x, meta_tokens, norm_mix, w_in, w_pool_mix, pool_scale, w_pool_out, w_gate_up, b_gate,
              gla_norm, w_gla_out, w_out, norm_ffn, w_ff1, w_ff2, norm_final):
    B = x.shape[0]
    meta = jnp.broadcast_to(meta_tokens[None].astype(x.dtype), (B, N_META, D_MODEL))
    h = jnp.concatenate([meta, x], axis=1)
    for l in range(DEPTH):
        h = h + hybrid_mixer(rmsnorm(h, norm_mix[l]), w_in[l], w_pool_mix[l], pool_scale[l], w_pool_out[l],
                             w_gate_up[l], b_gate[l], gla_norm[l], w_gla_out[l], w_out[l])
        h = h + squared_relu_mlp(rmsnorm(h, norm_ffn[l]), w_ff1[l], w_ff2[l])
    return rmsnorm(h, norm_final)[:, N_META:]
```

```python
import functools

import jax
import jax.numpy as jnp
from jax import lax
from jax.experimental import pallas as pl
from jax.experimental.pallas import tpu as pltpu

F32 = jnp.float32
BF16 = jnp.bfloat16

D_MODEL = 1024
N_META = 16
CHUNK = 64
EPS = 1e-6
POOL_WIDTH = 512
POOL_GROUPS = 4
POOL_GROUP_DIM = POOL_WIDTH // POOL_GROUPS
POOL_WINDOWS = (2, 4, 8, 16)
MAX_WINDOW = max(POOL_WINDOWS)
GLA_HEADS = 4
GLA_DK = 64
GLA_DV = 128
GLA_KEY = GLA_HEADS * GLA_DK
GLA_VAL = GLA_HEADS * GLA_DV
GLA_GATE_RANK = 16
GLA_TAU = 16.0
D_FF = 4 * D_MODEL

V7X_LANES = 128
V7X_VMEM_BYTES = 64 * 1024 * 1024

OFF_A = 0
OFF_Q = OFF_A + POOL_WIDTH
OFF_K = OFF_Q + GLA_KEY
OFF_V = OFF_K + GLA_KEY
OFF_R = OFF_V + GLA_VAL
OFF_LR = OFF_R + GLA_VAL
N_MAIN = OFF_LR + V7X_LANES

SEQ_TILE = 512
ROW_TILE = 512


def _rmsnorm(x, g):
    return x * lax.rsqrt(jnp.mean(x * x, axis=-1, keepdims=True) + EPS) * g


def _dot(a, b):
    return jnp.dot(a, b, preferred_element_type=F32)


def _dot_nt(a, b):
    return lax.dot_general(a, b, (((1,), (1,)), ((), ())), preferred_element_type=F32)


def _dot_tn(a, b):
    return lax.dot_general(a, b, (((0,), (0,)), ((), ())), preferred_element_type=F32)


def _block_mask(shape, row_block, col_block):
    r = lax.broadcasted_iota(jnp.int32, shape, 0) // row_block
    c = lax.broadcasted_iota(jnp.int32, shape, 1) // col_block
    return r == c


def _chunk_cumsum(la):
    c, n = la.shape
    tri = (lax.broadcasted_iota(jnp.int32, (c, c), 1) <= lax.broadcasted_iota(jnp.int32, (c, c), 0)).astype(BF16)
    hi = la.astype(BF16)
    rem = la - hi.astype(F32)
    mid = rem.astype(BF16)
    lo = (rem - mid.astype(F32)).astype(BF16)
    s = _dot(tri, jnp.concatenate([hi, mid, lo], axis=1))
    return s[:, :n] + s[:, n:2 * n] + s[:, 2 * n:]


def _state_update(st, k, v, b, mask_t):
    b_end = b[-1:, :]
    kb = (k * jnp.exp(b_end - b)).astype(BF16)
    u = _dot_tn(v.astype(BF16), kb)
    return st * jnp.exp(b_end) + jnp.where(mask_t, u, 0.0)


def _log_decay(z_lr, wgu_ref, bg_ref):
    return jax.nn.log_sigmoid(_dot(z_lr.astype(BF16), wgu_ref[...]) + bg_ref[...]) * (1.0 / GLA_TAU)


def _meta_kernel(meta_ref, nmix_ref, wmain_ref, wgu_ref, bg_ref, a_ref, st_ref):
    u = _rmsnorm(meta_ref[...], nmix_ref[...]).astype(BF16)
    z = _dot(u, wmain_ref[...])
    a_ref[...] = z[:, OFF_A:OFF_A + POOL_WIDTH]
    la = _log_decay(z[:, OFF_LR:OFF_LR + V7X_LANES], wgu_ref, bg_ref)
    b = _chunk_cumsum(la)
    mask_t = _block_mask((GLA_VAL, GLA_KEY), GLA_DV, GLA_DK)
    st0 = jnp.zeros((GLA_VAL, GLA_KEY), F32)
    st_ref[...] = _state_update(st0, z[:, OFF_K:OFF_K + GLA_KEY], z[:, OFF_V:OFF_V + GLA_VAL], b, mask_t)


def _gla_chunk(q, k, v, la, st, masks):
    mask_kk, mask_kv, mask_t, lower = masks
    b = _chunk_cumsum(la)
    d = b - b[CHUNK // 2 - 1:CHUNK // 2, :]
    e_pos = jnp.exp(d)
    e_neg = jnp.exp(-d)
    k_lo = jnp.concatenate([(k * e_neg).astype(BF16)] * GLA_HEADS, axis=0)
    k_up = jnp.concatenate([(k * e_pos).astype(BF16)] * GLA_HEADS, axis=0)
    zero = jnp.zeros((), BF16)
    a_lo = _dot_nt((q * e_pos).astype(BF16), jnp.where(mask_kk, k_lo, zero))
    a_up = _dot_nt((q * e_neg).astype(BF16), jnp.where(mask_kk, k_up, zero))
    a = jnp.where(lower, a_lo, a_up).astype(BF16)
    v_blk = jnp.where(mask_kv, jnp.concatenate([v.astype(BF16)] * GLA_HEADS, axis=0), zero)
    o = _dot(a, v_blk) + _dot_nt((q * jnp.exp(b)).astype(BF16), st.astype(BF16))
    return o, _state_update(st, k, v, b, mask_t)


def _mixer_kernel(x_ref, nmix_ref, wmain_ref, wg_ref, wpm_ref, pscale_ref, wpo_ref, wgu_ref, bg_ref, gnorm_ref,
                  wgo_ref, wout_ref, ameta_ref, st1_ref, h_ref, aext_ref, st_ref):
    rows = x_ref.shape[0]

    @pl.when(pl.program_id(1) == 0)
    def _():
        aext_ref[0:MAX_WINDOW, :] = ameta_ref[...]
        st_ref[...] = st1_ref[...]

    u = _rmsnorm(x_ref[...], nmix_ref[...]).astype(BF16)
    z = _dot(u, wmain_ref[...])

    aext_ref[MAX_WINDOW:MAX_WINDOW + rows, :] = z[:, OFF_A:OFF_A + POOL_WIDTH]
    pm = []
    for g, w in enumerate(POOL_WINDOWS):
        lanes = slice(g * POOL_GROUP_DIM, (g + 1) * POOL_GROUP_DIM)
        acc = aext_ref[MAX_WINDOW:MAX_WINDOW + rows, lanes]
        cur = acc
        for j in range(1, w):
            acc = acc + aext_ref[MAX_WINDOW - j:MAX_WINDOW - j + rows, lanes]
        p = acc * (1.0 / w) - cur
        pm.append(_dot(p.astype(BF16), wpm_ref[g]))
    aext_ref[0:MAX_WINDOW, :] = aext_ref[rows:rows + MAX_WINDOW, :]
    pm = (jnp.concatenate(pm, axis=1) * pscale_ref[...]).astype(BF16)
    y_a = _dot(pm, wpo_ref[...])

    la = _log_decay(z[:, OFF_LR:OFF_LR + V7X_LANES], wgu_ref, bg_ref)
    masks = (
        _block_mask((GLA_HEADS * CHUNK, GLA_KEY), CHUNK, GLA_DK),
        _block_mask((GLA_HEADS * CHUNK, GLA_VAL), CHUNK, GLA_DV),
        _block_mask((GLA_VAL, GLA_KEY), GLA_DV, GLA_DK),
        (lax.broadcasted_iota(jnp.int32, (CHUNK, GLA_HEADS * CHUNK), 1) % CHUNK)
        <= lax.broadcasted_iota(jnp.int32, (CHUNK, GLA_HEADS * CHUNK), 0),
    )
    st = st_ref[...]
    o_chunks = []
    for c in range(rows // CHUNK):
        rs = slice(c * CHUNK, (c + 1) * CHUNK)
        o_c, st = _gla_chunk(z[rs, OFF_Q:OFF_Q + GLA_KEY] * (GLA_DK ** -0.5), z[rs, OFF_K:OFF_K + GLA_KEY],
                             z[rs, OFF_V:OFF_V + GLA_VAL], la[rs, :], st, masks)
        o_chunks.append(o_c)
    st_ref[...] = st
    o = jnp.concatenate(o_chunks, axis=0)
    gnorm = gnorm_ref[...]
    r = z[:, OFF_R:OFF_R + GLA_VAL]
    on = []
    for h in range(GLA_HEADS):
        lanes = slice(h * GLA_DV, (h + 1) * GLA_DV)
        on.append(_rmsnorm(o[:, lanes], gnorm[:, lanes]) * jax.nn.silu(r[:, lanes]))
    y_b = _dot(jnp.concatenate(on, axis=1).astype(BF16), wgo_ref[...])

    zg = _dot(u, wg_ref[...])
    m = jax.nn.sigmoid(zg[:, :D_MODEL]) * y_a + jax.nn.sigmoid(zg[:, D_MODEL:]) * y_b
    h_ref[...] = x_ref[...] + _dot(m.astype(BF16), wout_ref[...])


def _mlp_kernel(h_ref, nffn_ref, w1_ref, w2_ref, nfin_ref, o_ref):
    u = _rmsnorm(h_ref[...], nffn_ref[...]).astype(BF16)
    hid = jnp.square(jnp.maximum(_dot(u, w1_ref[...]), 0.0)).astype(BF16)
    h2 = h_ref[...] + _dot(hid, w2_ref[...])
    o_ref[...] = _rmsnorm(h2, nfin_ref[...])


def _resident(shape):
    nd = len(shape)
    return pl.BlockSpec(shape, lambda *_: (0,) * nd, pipeline_mode=pl.Buffered(1))


def kernel(x, meta_tokens, norm_mix, w_in, w_pool_mix, pool_scale, w_pool_out, w_gate_up, b_gate, gla_norm, w_gla_out,
           w_out, norm_ffn, w_ff1, w_ff2, norm_final):
    batch, seq, d_model = x.shape
    assert d_model == D_MODEL and norm_mix.shape[0] == 1 and meta_tokens.shape[0] == N_META
    assert seq % SEQ_TILE == 0 and SEQ_TILE % CHUNK == 0 and (batch * seq) % ROW_TILE == 0

    wi = w_in[0]
    o_lr = POOL_WIDTH + 2 * GLA_KEY + 2 * GLA_VAL
    w_main = jnp.concatenate(
        [wi[:, :o_lr + GLA_GATE_RANK], jnp.zeros((D_MODEL, V7X_LANES - GLA_GATE_RANK), wi.dtype)], axis=1).astype(BF16)
    w_g = wi[:, o_lr + GLA_GATE_RANK:].astype(BF16)
    w_gu = jnp.concatenate(
        [w_gate_up[0], jnp.zeros((V7X_LANES - GLA_GATE_RANK, GLA_KEY), w_gate_up.dtype)], axis=0).astype(BF16)
    w_pm = w_pool_mix[0].astype(BF16)
    w_po = w_pool_out[0].astype(BF16)
    w_go = w_gla_out[0].astype(BF16)
    w_o = w_out[0].astype(BF16)
    w_1 = w_ff1[0].astype(BF16)
    w_2 = w_ff2[0].astype(BF16)
    n_fin = norm_final.reshape(1, D_MODEL)

    a_meta, st1 = pl.pallas_call(
        _meta_kernel,
        out_shape=(jax.ShapeDtypeStruct((N_META, POOL_WIDTH), F32), jax.ShapeDtypeStruct((GLA_VAL, GLA_KEY), F32)),
        name="meta",
    )(meta_tokens, norm_mix, w_main, w_gu, b_gate)

    h1 = pl.pallas_call(
        _mixer_kernel,
        grid=(batch, seq // SEQ_TILE),
        in_specs=[
            pl.BlockSpec((None, SEQ_TILE, D_MODEL), lambda b, j: (b, j, 0)),
            _resident((1, D_MODEL)),
            _resident((D_MODEL, N_MAIN)),
            _resident((D_MODEL, 2 * D_MODEL)),
            _resident((POOL_GROUPS, POOL_GROUP_DIM, POOL_GROUP_DIM)),
            _resident((1, POOL_WIDTH)),
            _resident((POOL_WIDTH, D_MODEL)),
            _resident((V7X_LANES, GLA_KEY)),
            _resident((1, GLA_KEY)),
            _resident((1, GLA_VAL)),
            _resident((GLA_VAL, D_MODEL)),
            _resident((D_MODEL, D_MODEL)),
            _resident((N_META, POOL_WIDTH)),
            _resident((GLA_VAL, GLA_KEY)),
        ],
        out_specs=pl.BlockSpec((None, SEQ_TILE, D_MODEL), lambda b, j: (b, j, 0)),
        out_shape=jax.ShapeDtypeStruct((batch, seq, D_MODEL), F32),
        scratch_shapes=[pltpu.VMEM((SEQ_TILE + MAX_WINDOW, POOL_WIDTH), F32), pltpu.VMEM((GLA_VAL, GLA_KEY), F32)],
        compiler_params=pltpu.CompilerParams(dimension_semantics=("arbitrary", "arbitrary"),
                                             vmem_limit_bytes=V7X_VMEM_BYTES * 3 // 4),
        name="mixer",
    )(x, norm_mix, w_main, w_g, w_pm, pool_scale, w_po, w_gu, b_gate, gla_norm, w_go, w_o, a_meta, st1)

    n_rows = batch * seq
    out = pl.pallas_call(
        _mlp_kernel,
        grid=(n_rows // ROW_TILE,),
        in_specs=[
            pl.BlockSpec((ROW_TILE, D_MODEL), lambda i: (i, 0)),
            _resident((1, D_MODEL)),
            _resident((D_MODEL, D_FF)),
            _resident((D_FF, D_MODEL)),
            _resident((1, D_MODEL)),
        ],
        out_specs=pl.BlockSpec((ROW_TILE, D_MODEL), lambda i: (i, 0)),
        out_shape=jax.ShapeDtypeStruct((n_rows, D_MODEL), F32),
        compiler_params=pltpu.CompilerParams(dimension_semantics=("arbitrary",),
                                             vmem_limit_bytes=V7X_VMEM_BYTES * 3 // 4),
        name="mlp",
    )(h1.reshape(n_rows, D_MODEL), norm_ffn, w_1, w_2, n_fin)
    return out.reshape(batch, seq, D_MODEL)
```

```python
import jax
import jax.numpy as jnp
from jax import lax
from jax.experimental import pallas as pl
from jax.experimental.pallas import tpu as pltpu

F32 = jnp.float32
BF16 = jnp.bfloat16

D_MODEL = 1024
N_META = 16
CHUNK = 64
EPS = 1e-6
POOL_WIDTH = 512
POOL_GROUPS = 4
POOL_GROUP_DIM = POOL_WIDTH // POOL_GROUPS
POOL_WINDOWS = (2, 4, 8, 16)
MAX_WINDOW = max(POOL_WINDOWS)
GLA_HEADS = 4
GLA_DK = 64
GLA_DV = 128
GLA_KEY = GLA_HEADS * GLA_DK
GLA_VAL = GLA_HEADS * GLA_DV
GLA_GATE_RANK = 16
GLA_TAU = 16.0
D_FF = 4 * D_MODEL

V7X_MXU_COLS = 256
V7X_VMEM_BYTES = 64 * 1024 * 1024

OFF_LR = 0
OFF_Q = OFF_LR + V7X_MXU_COLS
OFF_K = OFF_Q + GLA_KEY
OFF_A = OFF_K + GLA_KEY
OFF_V = OFF_A + POOL_WIDTH
OFF_R = OFF_V + GLA_VAL
N_MAIN = OFF_R + GLA_VAL

SEQ_TILE = 512
ROW_TILE = 512
SUB = 16
N_SUB = CHUNK // SUB
POOL_PAD = 8
POOL_BASE = POOL_PAD + MAX_WINDOW


def _rmsnorm(x, g):
    return x * lax.rsqrt(jnp.mean(x * x, axis=-1, keepdims=True) + EPS) * g


def _twice_sigmoid_of_twice(x):
    return 1.0 + jnp.tanh(x)


def _dot(a, b):
    return jnp.dot(a, b, preferred_element_type=F32)


def _dot_nt(a, b):
    return lax.dot_general(a, b, (((1,), (1,)), ((), ())), preferred_element_type=F32)


def _chunk_cumsum(la):
    c, n = la.shape
    tri = (lax.broadcasted_iota(jnp.int32, (c, c), 1) <= lax.broadcasted_iota(jnp.int32, (c, c), 0)).astype(BF16)
    hi = la.astype(BF16)
    lo = (la - hi.astype(F32)).astype(BF16)
    s = _dot(tri, jnp.concatenate([hi, lo], axis=1))
    return s[:, :n] + s[:, n:]


def _log_decay(z_lr, wgu_ref, bg_ref):
    return jax.nn.log_sigmoid(_dot(z_lr.astype(BF16), wgu_ref[...]) + bg_ref[...]) * (1.0 / GLA_TAU)


def _state_increment(k, v, b):
    kb_t = (k * jnp.exp(b[-1:, :] - b)).T.astype(BF16)
    vb = v.astype(BF16)
    out = []
    for p in range(GLA_HEADS // 2):
        pair = _dot(kb_t[2 * p * GLA_DK:2 * (p + 1) * GLA_DK, :], vb[:, 2 * p * GLA_DV:2 * (p + 1) * GLA_DV])
        out += [pair[:GLA_DK, :GLA_DV], pair[GLA_DK:, GLA_DV:]]
    return jnp.concatenate(out, axis=0)


def _head_block_diag(blocks, dtype):
    r, c = blocks[0].shape
    zero = jnp.zeros((r, c), dtype)
    return jnp.concatenate(
        [jnp.concatenate([blocks[h] if g == h else zero for g in range(GLA_HEADS)], axis=1) for h in range(GLA_HEADS)],
        axis=0)


def _meta_kernel(meta_ref, nmix_ref, wmain_ref, wgu_ref, bg_ref, a_ref, st_ref):
    u = _rmsnorm(meta_ref[...], nmix_ref[...]).astype(BF16)
    z = _dot(u, wmain_ref[...])
    a_ref[...] = z[:, OFF_A:OFF_A + POOL_WIDTH]
    la = _log_decay(z[:, OFF_LR:OFF_Q], wgu_ref, bg_ref)
    st_ref[...] = _state_increment(z[:, OFF_K:OFF_K + GLA_KEY], z[:, OFF_V:OFF_V + GLA_VAL], _chunk_cumsum(la))


def _score_operands(q, k, b, head_mask):
    ends = [b[(i + 1) * SUB - 1:(i + 1) * SUB, :] for i in range(N_SUB)]
    rows = lambda e: jnp.broadcast_to(e, (SUB, GLA_KEY))
    own_end = jnp.concatenate([rows(e) for e in ends], axis=0)
    own_start = jnp.concatenate([jnp.zeros((SUB, GLA_KEY), F32)] + [rows(e) for e in ends[:-1]], axis=0)
    x_end = own_end - b
    x_start = b - own_start
    lhs_lo, lhs_up = [], []
    for g in range(N_SUB - 1):
        cut = (g + 1) * SUB
        lhs_lo.append(q[cut:, :] * jnp.exp(b[cut:, :] - ends[g]))
        lhs_up.append(q[:cut, :] * jnp.exp(ends[g] - b[:cut, :]))
    lhs_lo.append(q * jnp.exp(-x_end))
    lhs_up.append(q * jnp.exp(-x_start))
    k_lo = (k * jnp.exp(x_end)).astype(BF16)
    k_up = (k * jnp.exp(x_start)).astype(BF16)
    zero = jnp.zeros((), BF16)
    blockdiag = lambda kk: jnp.where(head_mask, jnp.concatenate([kk] * GLA_HEADS, axis=0), zero)
    stack = lambda parts: jnp.concatenate(parts, axis=0).astype(BF16)
    return stack(lhs_lo), blockdiag(k_lo), stack(lhs_up), blockdiag(k_up)


_LO_OFF = [sum(CHUNK - (h + 1) * SUB for h in range(g)) for g in range(N_SUB - 1)]
_UP_OFF = [sum((h + 1) * SUB for h in range(g)) for g in range(N_SUB - 1)]
_DIAG_OFF = sum((h + 1) * SUB for h in range(N_SUB - 1))


def _assemble_scores(r_lo, r_up, col_blk, lower_tri):
    out = []
    for i in range(N_SUB):
        d = _DIAG_OFF + i * SUB
        acc = jnp.where(lower_tri, r_lo[d:d + SUB, :], r_up[d:d + SUB, :])
        for j in range(N_SUB):
            if j < i:
                off = _LO_OFF[j] + (i - j - 1) * SUB
                acc = jnp.where(col_blk == j, r_lo[off:off + SUB, :], acc)
            elif j > i:
                off = _UP_OFF[j - 1] + i * SUB
                acc = jnp.where(col_blk == j, r_up[off:off + SUB, :], acc)
        out.append(acc)
    return jnp.concatenate(out, axis=0)


def _trailing_means(aext_ref, sa_ref, sb_ref, rows):
    assert POOL_WINDOWS == (2, 4, 8, 16) and POOL_PAD == 8
    end = POOL_BASE + rows
    lane = lambda g: slice(g * POOL_GROUP_DIM, POOL_WIDTH)
    one = lambda g: slice(g * POOL_GROUP_DIM, (g + 1) * POOL_GROUP_DIM)
    sa_ref[POOL_PAD:end, :] = aext_ref[POOL_PAD:end, :] + aext_ref[POOL_PAD - 1:end - 1, :]
    sb_ref[POOL_PAD:end, lane(1)] = sa_ref[POOL_PAD:end, lane(1)] + sa_ref[POOL_PAD - 2:end - 2, lane(1)]
    sa_ref[POOL_PAD:end, lane(2)] = sb_ref[POOL_PAD:end, lane(2)] + sb_ref[POOL_PAD - 4:end - 4, lane(2)]
    sums = [sa_ref[POOL_BASE:end, one(0)], sb_ref[POOL_BASE:end, one(1)], sa_ref[POOL_BASE:end, one(2)],
            sa_ref[POOL_BASE:end, one(3)] + sa_ref[POOL_BASE - 8:end - 8, one(3)]]
    return [(sums[g] * (1.0 / w) - aext_ref[POOL_BASE:end, one(g)]).astype(BF16)
            for g, w in enumerate(POOL_WINDOWS)]


def _mixer_kernel(x_ref, nmix_ref, wmain_ref, wg_ref, wpm_ref, pscale_ref, wpo_ref, wgu_ref, bg_ref, gnorm_ref,
                  wgo_ref, wout_ref, ameta_ref, st1_ref, h_ref, aext_ref, psum_a_ref, psum_b_ref, st_ref):
    assert N_SUB == 4
    rows = x_ref.shape[0]
    n_chunks = rows // CHUNK
    chunk = lambda arr, c: arr[c * CHUNK:(c + 1) * CHUNK, :]

    @pl.when(pl.program_id(1) == 0)
    def _():
        zero_pad = jnp.zeros((POOL_PAD, POOL_WIDTH), F32)
        aext_ref[0:POOL_PAD, :] = zero_pad
        psum_a_ref[0:POOL_PAD, :] = zero_pad
        psum_b_ref[0:POOL_PAD, :] = zero_pad
        aext_ref[POOL_PAD:POOL_BASE, :] = ameta_ref[...]
        st_ref[...] = st1_ref[...]

    u = _rmsnorm(x_ref[...], nmix_ref[...]).astype(BF16)
    proj = lambda lo, hi: _dot(u, wmain_ref[:, lo:hi])

    z_lr = proj(OFF_LR, OFF_Q)
    zqk = proj(OFF_Q, OFF_A)
    z_a = proj(OFF_A, OFF_V)
    la = _log_decay(z_lr, wgu_ref, bg_ref)
    v = proj(OFF_V, OFF_R)
    b = [_chunk_cumsum(chunk(la, c)) for c in range(n_chunks)]
    r = proj(OFF_R, N_MAIN)
    q = zqk[:, :GLA_KEY] * (GLA_DK ** -0.5)
    k = zqk[:, GLA_KEY:]

    aext_ref[POOL_BASE:POOL_BASE + rows, :] = z_a
    pooled = _trailing_means(aext_ref, psum_a_ref, psum_b_ref, rows)
    aext_ref[POOL_PAD:POOL_BASE, :] = aext_ref[rows + POOL_PAD:rows + POOL_BASE, :]

    ci = lax.broadcasted_iota(jnp.int32, (SUB, GLA_HEADS * CHUNK), 1) % CHUNK
    col_blk = ci // SUB
    lower_tri = ci % SUB <= lax.broadcasted_iota(jnp.int32, (SUB, GLA_HEADS * CHUNK), 0)
    head_mask = (lax.broadcasted_iota(jnp.int32, (GLA_HEADS * CHUNK, GLA_KEY), 0) // CHUNK
                 == lax.broadcasted_iota(jnp.int32, (GLA_HEADS * CHUNK, GLA_KEY), 1) // GLA_DK)
    operands = [_score_operands(chunk(q, c), chunk(k, c), b[c], head_mask) for c in range(n_chunks)]
    half = D_MODEL // 2
    raw, incr = [], []

    def scores_and_increments(c0, c1):
        for c in range(c0, c1):
            lhs_lo, rhs_lo, lhs_up, rhs_up = operands[c]
            raw.append((_dot_nt(lhs_lo, rhs_lo), _dot_nt(lhs_up, rhs_up)))
        for c in range(c0, c1):
            incr.append(_state_increment(chunk(k, c), chunk(v, c), b[c]))

    group = n_chunks // 4
    scores_and_increments(0, group)
    pm = (_dot(jnp.concatenate(pooled, axis=1), wpm_ref[...]) * pscale_ref[...]).astype(BF16)
    scores_and_increments(group, 2 * group)
    y_a0 = _dot(pm, wpo_ref[:, :half])
    scores_and_increments(2 * group, 3 * group)
    y_a1 = _dot(pm, wpo_ref[:, half:])
    scores_and_increments(3 * group, n_chunks)
    zg_a0 = _dot(u, wg_ref[:, :half])

    st = st_ref[...]
    o_chunks = []

    def outputs(st, c0, c1):
        for c in range(c0, c1):
            a = _assemble_scores(raw[c][0], raw[c][1], col_blk, lower_tri).astype(BF16)
            vc = chunk(v, c).astype(BF16)
            v_blk = _head_block_diag([vc[:, h * GLA_DV:(h + 1) * GLA_DV] for h in range(GLA_HEADS)], BF16)
            stb = st.astype(BF16)
            s_blk = _head_block_diag([stb[h * GLA_DK:(h + 1) * GLA_DK, :] for h in range(GLA_HEADS)], BF16)
            qb = (chunk(q, c) * jnp.exp(b[c])).astype(BF16)
            o_chunks.append(_dot(a, v_blk) + _dot(qb, s_blk))
            decay = jnp.exp(b[c][CHUNK - 8:CHUNK, :].T[:, 7:8])
            st = st * decay + incr[c]
        return st

    st = outputs(st, 0, n_chunks // 2)
    zg_a1 = _dot(u, wg_ref[:, half:D_MODEL])
    st = outputs(st, n_chunks // 2, n_chunks)
    st_ref[...] = st
    o = jnp.concatenate(o_chunks, axis=0)
    y_a = jnp.concatenate([y_a0, y_a1], axis=1)
    zg_a = jnp.concatenate([zg_a0, zg_a1], axis=1)

    zg_b0 = _dot(u, wg_ref[:, D_MODEL:D_MODEL + half])
    gnorm = gnorm_ref[...]
    on = []
    for h in range(GLA_HEADS):
        lanes = slice(h * GLA_DV, (h + 1) * GLA_DV)
        rh = r[:, lanes]
        on.append(_rmsnorm(o[:, lanes], gnorm[:, lanes]) * (rh * _twice_sigmoid_of_twice(rh)))
    m_a = _twice_sigmoid_of_twice(zg_a) * y_a
    y_b = _dot(jnp.concatenate(on, axis=1).astype(BF16), wgo_ref[...])
    zg_b1 = _dot(u, wg_ref[:, D_MODEL + half:])

    m = m_a + _twice_sigmoid_of_twice(jnp.concatenate([zg_b0, zg_b1], axis=1)) * y_b
    h_ref[...] = x_ref[...] + _dot(m.astype(BF16), wout_ref[...])


def _mlp_kernel(h_ref, nffn_ref, w1_ref, w2_ref, nfin_ref, o_ref):
    u = _rmsnorm(h_ref[...], nffn_ref[...]).astype(BF16)
    hid = jnp.square(jnp.maximum(_dot(u, w1_ref[...]), 0.0)).astype(BF16)
    h2 = h_ref[...] + _dot(hid, w2_ref[...])
    o_ref[...] = _rmsnorm(h2, nfin_ref[...])


def _resident(shape):
    nd = len(shape)
    return pl.BlockSpec(shape, lambda *_: (0,) * nd, pipeline_mode=pl.Buffered(1))


def kernel(x, meta_tokens, norm_mix, w_in, w_pool_mix, pool_scale, w_pool_out, w_gate_up, b_gate, gla_norm, w_gla_out,
           w_out, norm_ffn, w_ff1, w_ff2, norm_final):
    batch, seq, d_model = x.shape
    assert d_model == D_MODEL and norm_mix.shape[0] == 1 and meta_tokens.shape[0] == N_META
    assert seq % SEQ_TILE == 0 and SEQ_TILE % CHUNK == 0 and (batch * seq) % ROW_TILE == 0

    wi = w_in[0]
    c_q = POOL_WIDTH
    c_v = c_q + 2 * GLA_KEY
    c_lr = c_v + 2 * GLA_VAL
    c_g = c_lr + GLA_GATE_RANK
    c_r = c_v + GLA_VAL
    w_main = jnp.concatenate(
        [wi[:, c_lr:c_g], jnp.zeros((D_MODEL, V7X_MXU_COLS - GLA_GATE_RANK), wi.dtype),
         wi[:, c_q:c_v], wi[:, :c_q], wi[:, c_v:c_r], 0.5 * wi[:, c_r:c_lr]], axis=1).astype(BF16)
    w_g = (0.5 * wi[:, c_g:]).astype(BF16)
    w_gu = jnp.concatenate(
        [w_gate_up[0], jnp.zeros((V7X_MXU_COLS - GLA_GATE_RANK, GLA_KEY), w_gate_up.dtype)], axis=0).astype(BF16)
    zero_blk = jnp.zeros((POOL_GROUP_DIM, POOL_GROUP_DIM), w_pool_mix.dtype)
    w_pm = jnp.concatenate(
        [jnp.concatenate([w_pool_mix[0, g] if h == g else zero_blk for h in range(POOL_GROUPS)], axis=1)
         for g in range(POOL_GROUPS)], axis=0).astype(BF16)
    w_po = w_pool_out[0].astype(BF16)
    w_go = w_gla_out[0].astype(BF16)
    w_o = (0.5 * w_out[0]).astype(BF16)
    w_1 = w_ff1[0].astype(BF16)
    w_2 = w_ff2[0].astype(BF16)
    n_fin = norm_final.reshape(1, D_MODEL)

    a_meta, st1 = pl.pallas_call(
        _meta_kernel,
        out_shape=(jax.ShapeDtypeStruct((N_META, POOL_WIDTH), F32), jax.ShapeDtypeStruct((GLA_KEY, GLA_DV), F32)),
        name="meta",
    )(meta_tokens, norm_mix, w_main, w_gu, b_gate)

    h1 = pl.pallas_call(
        _mixer_kernel,
        grid=(batch, seq // SEQ_TILE),
        in_specs=[
            pl.BlockSpec((None, SEQ_TILE, D_MODEL), lambda b, j: (b, j, 0)),
            _resident((1, D_MODEL)),
            _resident((D_MODEL, N_MAIN)),
            _resident((D_MODEL, 2 * D_MODEL)),
            _resident((POOL_WIDTH, POOL_WIDTH)),
            _resident((1, POOL_WIDTH)),
            _resident((POOL_WIDTH, D_MODEL)),
            _resident((V7X_MXU_COLS, GLA_KEY)),
            _resident((1, GLA_KEY)),
            _resident((1, GLA_VAL)),
            _resident((GLA_VAL, D_MODEL)),
            _resident((D_MODEL, D_MODEL)),
            _resident((N_META, POOL_WIDTH)),
            _resident((GLA_KEY, GLA_DV)),
        ],
        out_specs=pl.BlockSpec((None, SEQ_TILE, D_MODEL), lambda b, j: (b, j, 0)),
        out_shape=jax.ShapeDtypeStruct((batch, seq, D_MODEL), F32),
        scratch_shapes=[pltpu.VMEM((POOL_BASE + SEQ_TILE, POOL_WIDTH), F32)] * 3 + [pltpu.VMEM((GLA_KEY, GLA_DV), F32)],
        compiler_params=pltpu.CompilerParams(dimension_semantics=("arbitrary", "arbitrary"),
                                             vmem_limit_bytes=V7X_VMEM_BYTES * 3 // 4),
        name="mixer",
    )(x, norm_mix, w_main, w_g, w_pm, pool_scale, w_po, w_gu, b_gate, gla_norm, w_go, w_o, a_meta, st1)

    n_rows = batch * seq
    out = pl.pallas_call(
        _mlp_kernel,
        grid=(n_rows // ROW_TILE,),
        in_specs=[
            pl.BlockSpec((ROW_TILE, D_MODEL), lambda i: (i, 0)),
            _resident((1, D_MODEL)),
            _resident((D_MODEL, D_FF)),
            _resident((D_FF, D_MODEL)),
            _resident((1, D_MODEL)),
        ],
        out_specs=pl.BlockSpec((ROW_TILE, D_MODEL), lambda i: (i, 0)),
        out_shape=jax.ShapeDtypeStruct((n_rows, D_MODEL), F32),
        compiler_params=pltpu.CompilerParams(dimension_semantics=("arbitrary",),
                                             vmem_limit_bytes=V7X_VMEM_BYTES * 3 // 4),
        name="mlp",
    )(h1.reshape(n_rows, D_MODEL), norm_ffn, w_1, w_2, n_fin)
    return out.reshape(batch, seq, D_MODEL)
```

```python
import jax
import jax.numpy as jnp
from jax import lax
from jax.experimental import pallas as pl
from jax.experimental.pallas import tpu as pltpu

F32 = jnp.float32
BF16 = jnp.bfloat16

D_MODEL = 1024
N_META = 16
CHUNK = 64
EPS = 1e-6
POOL_WIDTH = 512
POOL_GROUPS = 4
POOL_GROUP_DIM = POOL_WIDTH // POOL_GROUPS
POOL_WINDOWS = (2, 4, 8, 16)
MAX_WINDOW = max(POOL_WINDOWS)
GLA_HEADS = 4
GLA_DK = 64
GLA_DV = 128
GLA_KEY = GLA_HEADS * GLA_DK
GLA_VAL = GLA_HEADS * GLA_DV
GLA_GATE_RANK = 16
GLA_TAU = 16.0
D_FF = 4 * D_MODEL

V7X_MXU_COLS = 256
V7X_VMEM_BYTES = 64 * 1024 * 1024

OFF_LR = 0
OFF_Q = OFF_LR + V7X_MXU_COLS
OFF_K = OFF_Q + GLA_KEY
OFF_A = OFF_K + GLA_KEY
OFF_V = OFF_A + POOL_WIDTH
OFF_R = OFF_V + GLA_VAL
N_MAIN = OFF_R + GLA_VAL

SEQ_TILE = 1024
ROW_TILE = 512
SUB = 16
N_SUB = CHUNK // SUB
POOL_PAD = 8
POOL_BASE = POOL_PAD + MAX_WINDOW


def _rmsnorm(x, g):
    return x * lax.rsqrt(jnp.mean(x * x, axis=-1, keepdims=True) + EPS) * g


def _twice_sigmoid_of_twice(x):
    return 1.0 + jnp.tanh(x)


def _dot(a, b):
    return jnp.dot(a, b, preferred_element_type=F32)


def _dot_nt(a, b):
    return lax.dot_general(a, b, (((1,), (1,)), ((), ())), preferred_element_type=F32)


def _chunk_cumsum(la):
    c, n = la.shape
    tri = (lax.broadcasted_iota(jnp.int32, (c, c), 1) <= lax.broadcasted_iota(jnp.int32, (c, c), 0)).astype(BF16)
    hi = la.astype(BF16)
    lo = (la - hi.astype(F32)).astype(BF16)
    s = _dot(tri, jnp.concatenate([hi, lo], axis=1))
    return s[:, :n] + s[:, n:]


def _log_decay(z_lr, wgu_ref, bg_ref):
    return jax.nn.log_sigmoid(_dot(z_lr.astype(BF16), wgu_ref[...]) + bg_ref[...]) * (1.0 / GLA_TAU)


def _state_increment(k, v, b):
    kb_t = (k * jnp.exp(b[-1:, :] - b)).T.astype(BF16)
    vb = v.astype(BF16)
    out = []
    for p in range(GLA_HEADS // 2):
        pair = _dot(kb_t[2 * p * GLA_DK:2 * (p + 1) * GLA_DK, :], vb[:, 2 * p * GLA_DV:2 * (p + 1) * GLA_DV])
        out += [pair[:GLA_DK, :GLA_DV], pair[GLA_DK:, GLA_DV:]]
    return jnp.concatenate(out, axis=0)


def _head_block_diag(blocks, dtype):
    r, c = blocks[0].shape
    zero = jnp.zeros((r, c), dtype)
    return jnp.concatenate(
        [jnp.concatenate([blocks[h] if g == h else zero for g in range(GLA_HEADS)], axis=1) for h in range(GLA_HEADS)],
        axis=0)


def _meta_kernel(meta_ref, nmix_ref, wmain_ref, wgu_ref, bg_ref, a_ref, st_ref):
    u = _rmsnorm(meta_ref[...], nmix_ref[...]).astype(BF16)
    z = _dot(u, wmain_ref[...])
    a_ref[...] = z[:, OFF_A:OFF_A + POOL_WIDTH]
    la = _log_decay(z[:, OFF_LR:OFF_Q], wgu_ref, bg_ref)
    st_ref[...] = _state_increment(z[:, OFF_K:OFF_K + GLA_KEY], z[:, OFF_V:OFF_V + GLA_VAL], _chunk_cumsum(la))


def _score_operands(q, k, b, head_mask):
    ends = [b[(i + 1) * SUB - 1:(i + 1) * SUB, :] for i in range(N_SUB)]
    rows = lambda e: jnp.broadcast_to(e, (SUB, GLA_KEY))
    own_end = jnp.concatenate([rows(e) for e in ends], axis=0)
    own_start = jnp.concatenate([jnp.zeros((SUB, GLA_KEY), F32)] + [rows(e) for e in ends[:-1]], axis=0)
    x_end = own_end - b
    x_start = b - own_start
    lhs_lo, lhs_up = [], []
    for g in range(N_SUB - 1):
        cut = (g + 1) * SUB
        lhs_lo.append(q[cut:, :] * jnp.exp(b[cut:, :] - ends[g]))
        lhs_up.append(q[:cut, :] * jnp.exp(ends[g] - b[:cut, :]))
    lhs_lo.append(q * jnp.exp(-x_end))
    lhs_up.append(q * jnp.exp(-x_start))
    k_lo = (k * jnp.exp(x_end)).astype(BF16)
    k_up = (k * jnp.exp(x_start)).astype(BF16)
    zero = jnp.zeros((), BF16)
    blockdiag = lambda kk: jnp.where(head_mask, jnp.concatenate([kk] * GLA_HEADS, axis=0), zero)
    stack = lambda parts: jnp.concatenate(parts, axis=0).astype(BF16)
    return stack(lhs_lo), blockdiag(k_lo), stack(lhs_up), blockdiag(k_up)


_LO_OFF = [sum(CHUNK - (h + 1) * SUB for h in range(g)) for g in range(N_SUB - 1)]
_UP_OFF = [sum((h + 1) * SUB for h in range(g)) for g in range(N_SUB - 1)]
_DIAG_OFF = sum((h + 1) * SUB for h in range(N_SUB - 1))


def _assemble_scores(r_lo, r_up, col_blk, lower_tri):
    out = []
    for i in range(N_SUB):
        d = _DIAG_OFF + i * SUB
        acc = jnp.where(lower_tri, r_lo[d:d + SUB, :], r_up[d:d + SUB, :])
        for j in range(N_SUB):
            if j < i:
                off = _LO_OFF[j] + (i - j - 1) * SUB
                acc = jnp.where(col_blk == j, r_lo[off:off + SUB, :], acc)
            elif j > i:
                off = _UP_OFF[j - 1] + i * SUB
                acc = jnp.where(col_blk == j, r_up[off:off + SUB, :], acc)
        out.append(acc)
    return jnp.concatenate(out, axis=0)


def _trailing_means(aext_ref, sa_ref, sb_ref, rows):
    assert POOL_WINDOWS == (2, 4, 8, 16) and POOL_PAD == 8
    end = POOL_BASE + rows
    lane = lambda g: slice(g * POOL_GROUP_DIM, POOL_WIDTH)
    one = lambda g: slice(g * POOL_GROUP_DIM, (g + 1) * POOL_GROUP_DIM)
    sa_ref[POOL_PAD:end, :] = aext_ref[POOL_PAD:end, :] + aext_ref[POOL_PAD - 1:end - 1, :]
    sb_ref[POOL_PAD:end, lane(1)] = sa_ref[POOL_PAD:end, lane(1)] + sa_ref[POOL_PAD - 2:end - 2, lane(1)]
    sa_ref[POOL_PAD:end, lane(2)] = sb_ref[POOL_PAD:end, lane(2)] + sb_ref[POOL_PAD - 4:end - 4, lane(2)]
    sums = [sa_ref[POOL_BASE:end, one(0)], sb_ref[POOL_BASE:end, one(1)], sa_ref[POOL_BASE:end, one(2)],
            sa_ref[POOL_BASE:end, one(3)] + sa_ref[POOL_BASE - 8:end - 8, one(3)]]
    return [(sums[g] * (1.0 / w) - aext_ref[POOL_BASE:end, one(g)]).astype(BF16)
            for g, w in enumerate(POOL_WINDOWS)]


def _mixer_kernel(x_ref, nmix_ref, wmain_ref, wg_ref, wpm_ref, pscale_ref, wpo_ref, wgu_ref, bg_ref, gnorm_ref,
                  wgo_ref, wout_ref, ameta_ref, st1_ref, h_ref, aext_ref, psum_a_ref, psum_b_ref, st_ref):
    assert N_SUB == 4
    rows = x_ref.shape[0]
    n_chunks = rows // CHUNK
    chunk = lambda arr, c: arr[c * CHUNK:(c + 1) * CHUNK, :]

    @pl.when(pl.program_id(1) == 0)
    def _():
        zero_pad = jnp.zeros((POOL_PAD, POOL_WIDTH), F32)
        aext_ref[0:POOL_PAD, :] = zero_pad
        psum_a_ref[0:POOL_PAD, :] = zero_pad
        psum_b_ref[0:POOL_PAD, :] = zero_pad
        aext_ref[POOL_PAD:POOL_BASE, :] = ameta_ref[...]
        st_ref[...] = st1_ref[...]

    u = _rmsnorm(x_ref[...], nmix_ref[...]).astype(BF16)
    proj = lambda lo, hi: _dot(u, wmain_ref[:, lo:hi])

    z_lr = proj(OFF_LR, OFF_Q)
    zqk = proj(OFF_Q, OFF_A)
    z_a = proj(OFF_A, OFF_V)
    la = _log_decay(z_lr, wgu_ref, bg_ref)
    v = proj(OFF_V, OFF_R)
    b = [_chunk_cumsum(chunk(la, c)) for c in range(n_chunks)]
    r = proj(OFF_R, N_MAIN)
    q = zqk[:, :GLA_KEY] * (GLA_DK ** -0.5)
    k = zqk[:, GLA_KEY:]

    aext_ref[POOL_BASE:POOL_BASE + rows, :] = z_a
    pooled = _trailing_means(aext_ref, psum_a_ref, psum_b_ref, rows)
    aext_ref[POOL_PAD:POOL_BASE, :] = aext_ref[rows + POOL_PAD:rows + POOL_BASE, :]

    ci = lax.broadcasted_iota(jnp.int32, (SUB, GLA_HEADS * CHUNK), 1) % CHUNK
    col_blk = ci // SUB
    lower_tri = ci % SUB <= lax.broadcasted_iota(jnp.int32, (SUB, GLA_HEADS * CHUNK), 0)
    head_mask = (lax.broadcasted_iota(jnp.int32, (GLA_HEADS * CHUNK, GLA_KEY), 0) // CHUNK
                 == lax.broadcasted_iota(jnp.int32, (GLA_HEADS * CHUNK, GLA_KEY), 1) // GLA_DK)
    operands = [_score_operands(chunk(q, c), chunk(k, c), b[c], head_mask) for c in range(n_chunks)]
    half = D_MODEL // 2
    raw, incr = [], []

    def scores_and_increments(c0, c1):
        for c in range(c0, c1):
            lhs_lo, rhs_lo, lhs_up, rhs_up = operands[c]
            raw.append((_dot_nt(lhs_lo, rhs_lo), _dot_nt(lhs_up, rhs_up)))
        for c in range(c0, c1):
            incr.append(_state_increment(chunk(k, c), chunk(v, c), b[c]))

    group = n_chunks // 4
    scores_and_increments(0, group)
    pm = (_dot(jnp.concatenate(pooled, axis=1), wpm_ref[...]) * pscale_ref[...]).astype(BF16)
    scores_and_increments(group, 2 * group)
    y_a0 = _dot(pm, wpo_ref[:, :half])
    scores_and_increments(2 * group, 3 * group)
    y_a1 = _dot(pm, wpo_ref[:, half:])
    scores_and_increments(3 * group, n_chunks)
    zg_a0 = _dot(u, wg_ref[:, :half])

    st = st_ref[...]
    o_chunks = []

    def outputs(st, c0, c1):
        for c in range(c0, c1):
            a = _assemble_scores(raw[c][0], raw[c][1], col_blk, lower_tri).astype(BF16)
            vc = chunk(v, c).astype(BF16)
            v_blk = _head_block_diag([vc[:, h * GLA_DV:(h + 1) * GLA_DV] for h in range(GLA_HEADS)], BF16)
            stb = st.astype(BF16)
            s_blk = _head_block_diag([stb[h * GLA_DK:(h + 1) * GLA_DK, :] for h in range(GLA_HEADS)], BF16)
            qb = (chunk(q, c) * jnp.exp(b[c])).astype(BF16)
            o_chunks.append(_dot(a, v_blk) + _dot(qb, s_blk))
            decay = jnp.exp(b[c][CHUNK - 8:CHUNK, :].T[:, 7:8])
            st = st * decay + incr[c]
        return st

    st = outputs(st, 0, n_chunks // 2)
    zg_a1 = _dot(u, wg_ref[:, half:D_MODEL])
    st = outputs(st, n_chunks // 2, n_chunks)
    st_ref[...] = st
    o = jnp.concatenate(o_chunks, axis=0)
    y_a = jnp.concatenate([y_a0, y_a1], axis=1)
    zg_a = jnp.concatenate([zg_a0, zg_a1], axis=1)

    zg_b0 = _dot(u, wg_ref[:, D_MODEL:D_MODEL + half])
    gnorm = gnorm_ref[...]
    on = []
    for h in range(GLA_HEADS):
        lanes = slice(h * GLA_DV, (h + 1) * GLA_DV)
        rh = r[:, lanes]
        on.append(_rmsnorm(o[:, lanes], gnorm[:, lanes]) * (rh * _twice_sigmoid_of_twice(rh)))
    m_a = _twice_sigmoid_of_twice(zg_a) * y_a
    y_b = _dot(jnp.concatenate(on, axis=1).astype(BF16), wgo_ref[...])
    zg_b1 = _dot(u, wg_ref[:, D_MODEL + half:])

    m = m_a + _twice_sigmoid_of_twice(jnp.concatenate([zg_b0, zg_b1], axis=1)) * y_b
    h_ref[...] = x_ref[...] + _dot(m.astype(BF16), wout_ref[...])


def _mlp_kernel(h_ref, nffn_ref, w1_ref, w2_ref, nfin_ref, o_ref):
    h = h_ref[...]
    c = lax.rsqrt(jnp.mean(h * h, axis=-1, keepdims=True) + EPS)
    hid = jnp.square(jnp.maximum(_dot((h * nffn_ref[...]).astype(BF16), w1_ref[...]), 0.0)).astype(BF16)
    half = h_ref.shape[0] // 2
    for lo in (0, half):
        rows = slice(lo, lo + half)
        h2 = h_ref[rows, :] + _dot(hid[rows, :], w2_ref[...]) * (c[rows, :] * c[rows, :])
        o_ref[rows, :] = _rmsnorm(h2, nfin_ref[...])


def _resident(shape):
    nd = len(shape)
    return pl.BlockSpec(shape, lambda *_: (0,) * nd, pipeline_mode=pl.Buffered(1))


def kernel(x, meta_tokens, norm_mix, w_in, w_pool_mix, pool_scale, w_pool_out, w_gate_up, b_gate, gla_norm, w_gla_out,
           w_out, norm_ffn, w_ff1, w_ff2, norm_final):
    batch, seq, d_model = x.shape
    assert d_model == D_MODEL and norm_mix.shape[0] == 1 and meta_tokens.shape[0] == N_META
    assert seq % SEQ_TILE == 0 and SEQ_TILE % CHUNK == 0 and (batch * seq) % ROW_TILE == 0

    wi = w_in[0]
    c_q = POOL_WIDTH
    c_v = c_q + 2 * GLA_KEY
    c_lr = c_v + 2 * GLA_VAL
    c_g = c_lr + GLA_GATE_RANK
    c_r = c_v + GLA_VAL
    w_main = jnp.concatenate(
        [wi[:, c_lr:c_g], jnp.zeros((D_MODEL, V7X_MXU_COLS - GLA_GATE_RANK), wi.dtype),
         wi[:, c_q:c_v], wi[:, :c_q], wi[:, c_v:c_r], 0.5 * wi[:, c_r:c_lr]], axis=1).astype(BF16)
    w_g = (0.5 * wi[:, c_g:]).astype(BF16)
    w_gu = jnp.concatenate(
        [w_gate_up[0], jnp.zeros((V7X_MXU_COLS - GLA_GATE_RANK, GLA_KEY), w_gate_up.dtype)], axis=0).astype(BF16)
    zero_blk = jnp.zeros((POOL_GROUP_DIM, POOL_GROUP_DIM), w_pool_mix.dtype)
    w_pm = jnp.concatenate(
        [jnp.concatenate([w_pool_mix[0, g] if h == g else zero_blk for h in range(POOL_GROUPS)], axis=1)
         for g in range(POOL_GROUPS)], axis=0).astype(BF16)
    w_po = w_pool_out[0].astype(BF16)
    w_go = w_gla_out[0].astype(BF16)
    w_o = (0.5 * w_out[0]).astype(BF16)
    w_1 = w_ff1[0].astype(BF16)
    w_2 = w_ff2[0].astype(BF16)
    n_fin = norm_final.reshape(1, D_MODEL)

    a_meta, st1 = pl.pallas_call(
        _meta_kernel,
        out_shape=(jax.ShapeDtypeStruct((N_META, POOL_WIDTH), F32), jax.ShapeDtypeStruct((GLA_KEY, GLA_DV), F32)),
        name="meta",
    )(meta_tokens, norm_mix, w_main, w_gu, b_gate)

    h1 = pl.pallas_call(
        _mixer_kernel,
        grid=(batch, seq // SEQ_TILE),
        in_specs=[
            pl.BlockSpec((None, SEQ_TILE, D_MODEL), lambda b, j: (b, j, 0)),
            _resident((1, D_MODEL)),
            _resident((D_MODEL, N_MAIN)),
            _resident((D_MODEL, 2 * D_MODEL)),
            _resident((POOL_WIDTH, POOL_WIDTH)),
            _resident((1, POOL_WIDTH)),
            _resident((POOL_WIDTH, D_MODEL)),
            _resident((V7X_MXU_COLS, GLA_KEY)),
            _resident((1, GLA_KEY)),
            _resident((1, GLA_VAL)),
            _resident((GLA_VAL, D_MODEL)),
            _resident((D_MODEL, D_MODEL)),
            _resident((N_META, POOL_WIDTH)),
            _resident((GLA_KEY, GLA_DV)),
        ],
        out_specs=pl.BlockSpec((None, SEQ_TILE, D_MODEL), lambda b, j: (b, j, 0)),
        out_shape=jax.ShapeDtypeStruct((batch, seq, D_MODEL), F32),
        scratch_shapes=[pltpu.VMEM((POOL_BASE + SEQ_TILE, POOL_WIDTH), F32)] * 3 + [pltpu.VMEM((GLA_KEY, GLA_DV), F32)],
        compiler_params=pltpu.CompilerParams(dimension_semantics=("arbitrary", "arbitrary"),
                                             vmem_limit_bytes=V7X_VMEM_BYTES * 3 // 4),
        name="mixer",
    )(x, norm_mix, w_main, w_g, w_pm, pool_scale, w_po, w_gu, b_gate, gla_norm, w_go, w_o, a_meta, st1)

    n_rows = batch * seq
    out = pl.pallas_call(
        _mlp_kernel,
        grid=(n_rows // ROW_TILE,),
        in_specs=[
            pl.BlockSpec((ROW_TILE, D_MODEL), lambda i: (i, 0)),
            _resident((1, D_MODEL)),
            _resident((D_MODEL, D_FF)),
            _resident((D_FF, D_MODEL)),
            _resident((1, D_MODEL)),
        ],
        out_specs=pl.BlockSpec((ROW_TILE, D_MODEL), lambda i: (i, 0)),
        out_shape=jax.ShapeDtypeStruct((n_rows, D_MODEL), F32),
        compiler_params=pltpu.CompilerParams(dimension_semantics=("arbitrary",),
                                             vmem_limit_bytes=V7X_VMEM_BYTES * 3 // 4),
        name="mlp",
    )(h1.reshape(n_rows, D_MODEL), norm_ffn, w_1, w_2, n_fin)
    return out.reshape(batch, seq, D_MODEL)
```

```python
import jax
import jax.numpy as jnp
from jax import lax
from jax.experimental import pallas as pl
from jax.experimental.pallas import tpu as pltpu

F32 = jnp.float32
BF16 = jnp.bfloat16

D_MODEL = 1024
N_META = 16
CHUNK = 64
EPS = 1e-6
POOL_WIDTH = 512
POOL_GROUPS = 4
POOL_GROUP_DIM = POOL_WIDTH // POOL_GROUPS
POOL_WINDOWS = (2, 4, 8, 16)
MAX_WINDOW = max(POOL_WINDOWS)
GLA_HEADS = 4
GLA_DK = 64
GLA_DV = 128
GLA_KEY = GLA_HEADS * GLA_DK
GLA_VAL = GLA_HEADS * GLA_DV
GLA_GATE_RANK = 16
GLA_TAU = 16.0
D_FF = 4 * D_MODEL

V7X_MXU_COLS = 256
V7X_VMEM_BYTES = 64 * 1024 * 1024

OFF_LR = 0
OFF_Q = OFF_LR + V7X_MXU_COLS
OFF_K = OFF_Q + GLA_KEY
OFF_A = OFF_K + GLA_KEY
OFF_V = OFF_A + POOL_WIDTH
OFF_R = OFF_V + GLA_VAL
N_MAIN = OFF_R + GLA_VAL

SEQ_TILE = 1024
ROW_TILE = 1024
MLP_OUT_ROWS = 256
SUB = 16
N_SUB = CHUNK // SUB
POOL_PAD = 8
POOL_BASE = POOL_PAD + MAX_WINDOW


def _rmsnorm(x, g):
    return x * lax.rsqrt(jnp.mean(x * x, axis=-1, keepdims=True) + EPS) * g


def _twice_sigmoid_of_twice(x):
    return 1.0 + jnp.tanh(x)


def _dot(a, b):
    return jnp.dot(a, b, preferred_element_type=F32)


def _dot_nt(a, b):
    return lax.dot_general(a, b, (((1,), (1,)), ((), ())), preferred_element_type=F32)


def _chunk_cumsum(la):
    c, n = la.shape
    tri = (lax.broadcasted_iota(jnp.int32, (c, c), 1) <= lax.broadcasted_iota(jnp.int32, (c, c), 0)).astype(BF16)
    hi = la.astype(BF16)
    lo = (la - hi.astype(F32)).astype(BF16)
    s = _dot(tri, jnp.concatenate([hi, lo], axis=1))
    return s[:, :n] + s[:, n:]


def _log_decay(z_lr, wgu_ref, bg_ref):
    return jax.nn.log_sigmoid(_dot(z_lr.astype(BF16), wgu_ref[...]) + bg_ref[...]) * (1.0 / GLA_TAU)


def _state_increment(k, v, b):
    kb_t = (k * jnp.exp(b[-1:, :] - b)).T.astype(BF16)
    vb = v.astype(BF16)
    out = []
    for p in range(GLA_HEADS // 2):
        pair = _dot(kb_t[2 * p * GLA_DK:2 * (p + 1) * GLA_DK, :], vb[:, 2 * p * GLA_DV:2 * (p + 1) * GLA_DV])
        out += [pair[:GLA_DK, :GLA_DV], pair[GLA_DK:, GLA_DV:]]
    return jnp.concatenate(out, axis=0)


def _block_diag(blocks):
    n = len(blocks)
    zero = jnp.zeros(blocks[0].shape, blocks[0].dtype)
    return jnp.concatenate(
        [jnp.concatenate([blocks[i] if j == i else zero for j in range(n)], axis=1) for i in range(n)], axis=0)


def _meta_kernel(meta_ref, nmix_ref, wmain_ref, wgu_ref, bg_ref, a_ref, st_ref):
    u = _rmsnorm(meta_ref[...], nmix_ref[...]).astype(BF16)
    z = _dot(u, wmain_ref[...])
    a_ref[...] = z[:, OFF_A:OFF_A + POOL_WIDTH]
    la = _log_decay(z[:, OFF_LR:OFF_Q], wgu_ref, bg_ref)
    st_ref[...] = _state_increment(z[:, OFF_K:OFF_K + GLA_KEY], z[:, OFF_V:OFF_V + GLA_VAL], _chunk_cumsum(la))


def _score_operands(q, k, b, head_mask):
    ends = [b[(i + 1) * SUB - 1:(i + 1) * SUB, :] for i in range(N_SUB)]
    rows = lambda e: jnp.broadcast_to(e, (SUB, GLA_KEY))
    own_end = jnp.concatenate([rows(e) for e in ends], axis=0)
    own_start = jnp.concatenate([jnp.zeros((SUB, GLA_KEY), F32)] + [rows(e) for e in ends[:-1]], axis=0)
    x_end = own_end - b
    x_start = b - own_start
    lhs_lo, lhs_up = [], []
    for g in range(N_SUB - 1):
        cut = (g + 1) * SUB
        lhs_lo.append(q[cut:, :] * jnp.exp(b[cut:, :] - ends[g]))
        lhs_up.append(q[:cut, :] * jnp.exp(ends[g] - b[:cut, :]))
    lhs_lo.append(q * jnp.exp(-x_end))
    lhs_up.append(q * jnp.exp(-x_start))
    k_lo = (k * jnp.exp(x_end)).astype(BF16)
    k_up = (k * jnp.exp(x_start)).astype(BF16)
    zero = jnp.zeros((), BF16)
    blockdiag = lambda kk: jnp.where(head_mask, jnp.concatenate([kk] * GLA_HEADS, axis=0), zero)
    stack = lambda parts: jnp.concatenate(parts, axis=0).astype(BF16)
    return stack(lhs_lo), blockdiag(k_lo), stack(lhs_up), blockdiag(k_up)


_LO_OFF = [sum(CHUNK - (h + 1) * SUB for h in range(g)) for g in range(N_SUB - 1)]
_UP_OFF = [sum((h + 1) * SUB for h in range(g)) for g in range(N_SUB - 1)]
_DIAG_OFF = sum((h + 1) * SUB for h in range(N_SUB - 1))


def _assemble_scores(r_lo, r_up, col_blk, lower_tri):
    out = []
    for i in range(N_SUB):
        d = _DIAG_OFF + i * SUB
        acc = jnp.where(lower_tri, r_lo[d:d + SUB, :], r_up[d:d + SUB, :])
        for j in range(N_SUB):
            if j < i:
                off = _LO_OFF[j] + (i - j - 1) * SUB
                acc = jnp.where(col_blk == j, r_lo[off:off + SUB, :], acc)
            elif j > i:
                off = _UP_OFF[j - 1] + i * SUB
                acc = jnp.where(col_blk == j, r_up[off:off + SUB, :], acc)
        out.append(acc)
    return jnp.concatenate(out, axis=0)


def _trailing_means(aext_ref, sa_ref, sb_ref, rows):
    assert POOL_WINDOWS == (2, 4, 8, 16) and POOL_PAD == 8
    end = POOL_BASE + rows
    lane = lambda g: slice(g * POOL_GROUP_DIM, POOL_WIDTH)
    one = lambda g: slice(g * POOL_GROUP_DIM, (g + 1) * POOL_GROUP_DIM)
    sa_ref[POOL_PAD:end, :] = aext_ref[POOL_PAD:end, :] + aext_ref[POOL_PAD - 1:end - 1, :]
    sb_ref[POOL_PAD:end, lane(1)] = sa_ref[POOL_PAD:end, lane(1)] + sa_ref[POOL_PAD - 2:end - 2, lane(1)]
    sa_ref[POOL_PAD:end, lane(2)] = sb_ref[POOL_PAD:end, lane(2)] + sb_ref[POOL_PAD - 4:end - 4, lane(2)]
    sums = [sa_ref[POOL_BASE:end, one(0)], sb_ref[POOL_BASE:end, one(1)], sa_ref[POOL_BASE:end, one(2)],
            sa_ref[POOL_BASE:end, one(3)] + sa_ref[POOL_BASE - 8:end - 8, one(3)]]
    return [(sums[g] * (1.0 / w) - aext_ref[POOL_BASE:end, one(g)]).astype(BF16)
            for g, w in enumerate(POOL_WINDOWS)]


def _mixer_kernel(x_ref, nmix_ref, wmain_ref, wg_ref, wpm_ref, pscale_ref, wpo_ref, wgu_ref, bg_ref, gnorm_ref,
                  wgo_ref, wout_ref, ameta_ref, st1_ref, h_ref, aext_ref, psum_a_ref, psum_b_ref, st_ref):
    assert N_SUB == 4
    rows = x_ref.shape[0]
    n_chunks = rows // CHUNK
    chunk = lambda arr, c: arr[c * CHUNK:(c + 1) * CHUNK, :]

    @pl.when(pl.program_id(1) == 0)
    def _():
        zero_pad = jnp.zeros((POOL_PAD, POOL_WIDTH), F32)
        aext_ref[0:POOL_PAD, :] = zero_pad
        psum_a_ref[0:POOL_PAD, :] = zero_pad
        psum_b_ref[0:POOL_PAD, :] = zero_pad
        aext_ref[POOL_PAD:POOL_BASE, :] = ameta_ref[...]
        st_ref[...] = st1_ref[...]

    u = _rmsnorm(x_ref[...], nmix_ref[...]).astype(BF16)
    proj = lambda lo, hi: _dot(u, wmain_ref[:, lo:hi])

    z_lr = proj(OFF_LR, OFF_Q)
    zqk = proj(OFF_Q, OFF_A)
    z_a = proj(OFF_A, OFF_V)
    la = _log_decay(z_lr, wgu_ref, bg_ref)
    v = proj(OFF_V, OFF_R)
    b = [_chunk_cumsum(chunk(la, c)) for c in range(n_chunks)]
    r = proj(OFF_R, N_MAIN)
    q = zqk[:, :GLA_KEY] * (GLA_DK ** -0.5)
    k = zqk[:, GLA_KEY:]

    aext_ref[POOL_BASE:POOL_BASE + rows, :] = z_a
    pooled = _trailing_means(aext_ref, psum_a_ref, psum_b_ref, rows)
    aext_ref[POOL_PAD:POOL_BASE, :] = aext_ref[rows + POOL_PAD:rows + POOL_BASE, :]

    ci = lax.broadcasted_iota(jnp.int32, (SUB, GLA_HEADS * CHUNK), 1) % CHUNK
    col_blk = ci // SUB
    lower_tri = ci % SUB <= lax.broadcasted_iota(jnp.int32, (SUB, GLA_HEADS * CHUNK), 0)
    head_mask = (lax.broadcasted_iota(jnp.int32, (GLA_HEADS * CHUNK, GLA_KEY), 0) // CHUNK
                 == lax.broadcasted_iota(jnp.int32, (GLA_HEADS * CHUNK, GLA_KEY), 1) // GLA_DK)
    operands = [_score_operands(chunk(q, c), chunk(k, c), b[c], head_mask) for c in range(n_chunks)]
    half = D_MODEL // 2
    raw, incr = [], []

    def scores_and_increments(c0, c1):
        for c in range(c0, c1):
            lhs_lo, rhs_lo, lhs_up, rhs_up = operands[c]
            raw.append((_dot_nt(lhs_lo, rhs_lo), _dot_nt(lhs_up, rhs_up)))
            incr.append(_state_increment(chunk(k, c), chunk(v, c), b[c]))

    group = n_chunks // 4
    scores_and_increments(0, group)
    zg_a0 = _dot(u, wg_ref[:, :half])
    scores_and_increments(group, 2 * group)
    zg_a1 = _dot(u, wg_ref[:, half:D_MODEL])
    scores_and_increments(2 * group, 3 * group)
    pm = (_dot(jnp.concatenate(pooled, axis=1), wpm_ref[...]) * pscale_ref[...]).astype(BF16)
    y_a0 = _dot(pm, wpo_ref[:, :half])
    scores_and_increments(3 * group, n_chunks)
    y_a1 = _dot(pm, wpo_ref[:, half:])

    st = st_ref[...]
    o_chunks = []

    def outputs(st, c0, c1):
        for c in range(c0, c1):
            a = _assemble_scores(raw[c][0], raw[c][1], col_blk, lower_tri).astype(BF16)
            vc = chunk(v, c).astype(BF16)
            stb = st.astype(BF16)
            qb = (chunk(q, c) * jnp.exp(b[c])).astype(BF16)
            o_pairs = []
            for p in range(0, GLA_HEADS, 2):
                heads = (p, p + 1)
                v_blk = _block_diag([vc[:, h * GLA_DV:(h + 1) * GLA_DV] for h in heads])
                s_blk = _block_diag([stb[h * GLA_DK:(h + 1) * GLA_DK, :] for h in heads])
                o_pairs.append(_dot(a[:, p * CHUNK:(p + 2) * CHUNK], v_blk)
                               + _dot(qb[:, p * GLA_DK:(p + 2) * GLA_DK], s_blk))
            o_chunks.append(jnp.concatenate(o_pairs, axis=1))
            decay = jnp.exp(b[c][CHUNK - 8:CHUNK, :].T[:, 7:8])
            st = st * decay + incr[c]
        return st

    st = outputs(st, 0, n_chunks)
    st_ref[...] = st
    o = jnp.concatenate(o_chunks, axis=0)
    y_a = jnp.concatenate([y_a0, y_a1], axis=1)
    zg_a = jnp.concatenate([zg_a0, zg_a1], axis=1)

    zg_b0 = _dot(u, wg_ref[:, D_MODEL:D_MODEL + half])
    gnorm = gnorm_ref[...]
    on = []
    for h in range(GLA_HEADS):
        lanes = slice(h * GLA_DV, (h + 1) * GLA_DV)
        rh = r[:, lanes]
        on.append(_rmsnorm(o[:, lanes], gnorm[:, lanes]) * (rh * _twice_sigmoid_of_twice(rh)))
    m_a = _twice_sigmoid_of_twice(zg_a) * y_a
    y_b = _dot(jnp.concatenate(on, axis=1).astype(BF16), wgo_ref[...])
    zg_b1 = _dot(u, wg_ref[:, D_MODEL + half:])

    m = m_a + _twice_sigmoid_of_twice(jnp.concatenate([zg_b0, zg_b1], axis=1)) * y_b
    h_ref[...] = x_ref[...] + _dot(m.astype(BF16), wout_ref[...])


def _mlp_kernel(h_ref, nffn_ref, w1_ref, w2_ref, nfin_ref, o_ref):
    h = h_ref[...]
    c = lax.rsqrt(jnp.mean(h * h, axis=-1, keepdims=True) + EPS)
    hid = jnp.square(jnp.maximum(_dot((h * nffn_ref[...]).astype(BF16), w1_ref[...]), 0.0)).astype(BF16)
    for lo in range(0, h_ref.shape[0], MLP_OUT_ROWS):
        rows = slice(lo, lo + MLP_OUT_ROWS)
        h2 = h_ref[rows, :] + _dot(hid[rows, :], w2_ref[...]) * (c[rows, :] * c[rows, :])
        o_ref[rows, :] = _rmsnorm(h2, nfin_ref[...])


def _resident(shape):
    nd = len(shape)
    return pl.BlockSpec(shape, lambda *_: (0,) * nd, pipeline_mode=pl.Buffered(1))


def kernel(x, meta_tokens, norm_mix, w_in, w_pool_mix, pool_scale, w_pool_out, w_gate_up, b_gate, gla_norm, w_gla_out,
           w_out, norm_ffn, w_ff1, w_ff2, norm_final):
    batch, seq, d_model = x.shape
    assert d_model == D_MODEL and norm_mix.shape[0] == 1 and meta_tokens.shape[0] == N_META
    assert seq % SEQ_TILE == 0 and SEQ_TILE % CHUNK == 0 and (batch * seq) % ROW_TILE == 0

    wi = w_in[0]
    c_q = POOL_WIDTH
    c_v = c_q + 2 * GLA_KEY
    c_lr = c_v + 2 * GLA_VAL
    c_g = c_lr + GLA_GATE_RANK
    c_r = c_v + GLA_VAL
    w_main = jnp.concatenate(
        [wi[:, c_lr:c_g], jnp.zeros((D_MODEL, V7X_MXU_COLS - GLA_GATE_RANK), wi.dtype),
         wi[:, c_q:c_v], wi[:, :c_q], wi[:, c_v:c_r], 0.5 * wi[:, c_r:c_lr]], axis=1).astype(BF16)
    w_g = (0.5 * wi[:, c_g:]).astype(BF16)
    w_gu = jnp.concatenate(
        [w_gate_up[0], jnp.zeros((V7X_MXU_COLS - GLA_GATE_RANK, GLA_KEY), w_gate_up.dtype)], axis=0).astype(BF16)
    zero_blk = jnp.zeros((POOL_GROUP_DIM, POOL_GROUP_DIM), w_pool_mix.dtype)
    w_pm = jnp.concatenate(
        [jnp.concatenate([w_pool_mix[0, g] if h == g else zero_blk for h in range(POOL_GROUPS)], axis=1)
         for g in range(POOL_GROUPS)], axis=0).astype(BF16)
    w_po = w_pool_out[0].astype(BF16)
    w_go = w_gla_out[0].astype(BF16)
    w_o = (0.5 * w_out[0]).astype(BF16)
    w_1 = w_ff1[0].astype(BF16)
    w_2 = w_ff2[0].astype(BF16)
    n_fin = norm_final.reshape(1, D_MODEL)

    a_meta, st1 = pl.pallas_call(
        _meta_kernel,
        out_shape=(jax.ShapeDtypeStruct((N_META, POOL_WIDTH), F32), jax.ShapeDtypeStruct((GLA_KEY, GLA_DV), F32)),
        name="meta",
    )(meta_tokens, norm_mix, w_main, w_gu, b_gate)

    h1 = pl.pallas_call(
        _mixer_kernel,
        grid=(batch, seq // SEQ_TILE),
        in_specs=[
            pl.BlockSpec((None, SEQ_TILE, D_MODEL), lambda b, j: (b, j, 0)),
            _resident((1, D_MODEL)),
            _resident((D_MODEL, N_MAIN)),
            _resident((D_MODEL, 2 * D_MODEL)),
            _resident((POOL_WIDTH, POOL_WIDTH)),
            _resident((1, POOL_WIDTH)),
            _resident((POOL_WIDTH, D_MODEL)),
            _resident((V7X_MXU_COLS, GLA_KEY)),
            _resident((1, GLA_KEY)),
            _resident((1, GLA_VAL)),
            _resident((GLA_VAL, D_MODEL)),
            _resident((D_MODEL, D_MODEL)),
            _resident((N_META, POOL_WIDTH)),
            _resident((GLA_KEY, GLA_DV)),
        ],
        out_specs=pl.BlockSpec((None, SEQ_TILE, D_MODEL), lambda b, j: (b, j, 0)),
        out_shape=jax.ShapeDtypeStruct((batch, seq, D_MODEL), F32),
        scratch_shapes=[pltpu.VMEM((POOL_BASE + SEQ_TILE, POOL_WIDTH), F32)] * 3 + [pltpu.VMEM((GLA_KEY, GLA_DV), F32)],
        compiler_params=pltpu.CompilerParams(dimension_semantics=("arbitrary", "arbitrary"),
                                             vmem_limit_bytes=V7X_VMEM_BYTES * 3 // 4),
        name="mixer",
    )(x, norm_mix, w_main, w_g, w_pm, pool_scale, w_po, w_gu, b_gate, gla_norm, w_go, w_o, a_meta, st1)

    n_rows = batch * seq
    out = pl.pallas_call(
        _mlp_kernel,
        grid=(n_rows // ROW_TILE,),
        in_specs=[
            pl.BlockSpec((ROW_TILE, D_MODEL), lambda i: (i, 0)),
            _resident((1, D_MODEL)),
            _resident((D_MODEL, D_FF)),
            _resident((D_FF, D_MODEL)),
            _resident((1, D_MODEL)),
        ],
        out_specs=pl.BlockSpec((ROW_TILE, D_MODEL), lambda i: (i, 0)),
        out_shape=jax.ShapeDtypeStruct((n_rows, D_MODEL), F32),
        compiler_params=pltpu.CompilerParams(dimension_semantics=("arbitrary",),
                                             vmem_limit_bytes=V7X_VMEM_BYTES * 3 // 4),
        name="mlp",
    )(h1.reshape(n_rows, D_MODEL), norm_ffn, w_1, w_2, n_fin)
    return out.reshape(batch, seq, D_MODEL)
```

```python
import jax
import jax.numpy as jnp
import numpy as np
from jax import lax
from jax.experimental import pallas as pl
from jax.experimental.pallas import tpu as pltpu

F32 = jnp.float32
BF16 = jnp.bfloat16

D_MODEL = 1024
N_META = 16
CHUNK = 64
EPS = 1e-6
POOL_WIDTH = 512
POOL_GROUPS = 4
POOL_GROUP_DIM = POOL_WIDTH // POOL_GROUPS
POOL_WINDOWS = (2, 4, 8, 16)
MAX_WINDOW = max(POOL_WINDOWS)
GLA_HEADS = 4
GLA_DK = 64
GLA_DV = 128
GLA_KEY = GLA_HEADS * GLA_DK
GLA_VAL = GLA_HEADS * GLA_DV
GLA_GATE_RANK = 16
GLA_TAU = 16.0
D_FF = 4 * D_MODEL

V7X_VMEM_BYTES = 64 * 1024 * 1024

OFF_A = 0
OFF_Q = OFF_A + POOL_WIDTH
OFF_K = OFF_Q + GLA_KEY
OFF_V = OFF_K + GLA_KEY
OFF_R = OFF_V + GLA_VAL
N_MAIN = OFF_R + GLA_VAL

SEQ_TILE = 1024
ROW_TILE = 1024
MLP_OUT_ROWS = 256
SUB = 16
N_SUB = CHUNK // SUB
POOL_PAD = 8
POOL_BASE = POOL_PAD + MAX_WINDOW


def _rmsnorm(x, g):
    return x * lax.rsqrt(jnp.mean(x * x, axis=-1, keepdims=True) + EPS) * g


def _twice_sigmoid_of_twice(x):
    return 1.0 + jnp.tanh(x)


def _dot(a, b):
    return jnp.dot(a, b, preferred_element_type=F32)


def _dot_nt(a, b):
    return lax.dot_general(a, b, (((1,), (1,)), ((), ())), preferred_element_type=F32)


def _chunk_cumsum(la):
    c = la.shape[0]
    tri = (lax.broadcasted_iota(jnp.int32, (c, c), 1) <= lax.broadcasted_iota(jnp.int32, (c, c), 0)).astype(BF16)
    return _dot(tri, la.astype(BF16))


def _log_decay(gate_lin, bg_ref):
    return jax.nn.log_sigmoid(gate_lin + bg_ref[...]) * (1.0 / GLA_TAU)


def _state_increment(k, v, b):
    kb_t = (k * jnp.exp(b[-1:, :] - b)).T.astype(BF16)
    vb = v.astype(BF16)
    out = []
    for p in range(GLA_HEADS // 2):
        pair = _dot(kb_t[2 * p * GLA_DK:2 * (p + 1) * GLA_DK, :], vb[:, 2 * p * GLA_DV:2 * (p + 1) * GLA_DV])
        out += [pair[:GLA_DK, :GLA_DV], pair[GLA_DK:, GLA_DV:]]
    return jnp.concatenate(out, axis=0)


def _block_diag(blocks):
    n = len(blocks)
    zero = jnp.zeros(blocks[0].shape, blocks[0].dtype)
    return jnp.concatenate(
        [jnp.concatenate([blocks[i] if j == i else zero for j in range(n)], axis=1) for i in range(n)], axis=0)


def _meta_kernel(meta_ref, nmix_ref, wmain_ref, wlr_ref, wup_ref, bg_ref, a_ref, st_ref, wgate_ref):
    w_gate = jnp.dot(wlr_ref[...], wup_ref[...], preferred_element_type=F32,
                     precision=lax.Precision.HIGHEST).astype(BF16)
    wgate_ref[...] = w_gate
    u = _rmsnorm(meta_ref[...], nmix_ref[...]).astype(BF16)
    z = _dot(u, wmain_ref[...])
    a_ref[...] = z[:, OFF_A:OFF_A + POOL_WIDTH]
    la = _log_decay(_dot(u, w_gate), bg_ref)
    st_ref[...] = _state_increment(z[:, OFF_K:OFF_K + GLA_KEY], z[:, OFF_V:OFF_V + GLA_VAL], _chunk_cumsum(la))


def _score_operands(q, k, b, head_mask):
    ends = [b[(i + 1) * SUB - 1:(i + 1) * SUB, :] for i in range(N_SUB)]
    rows = lambda e: jnp.broadcast_to(e, (SUB, GLA_KEY))
    own_end = jnp.concatenate([rows(e) for e in ends], axis=0)
    own_start = jnp.concatenate([jnp.zeros((SUB, GLA_KEY), F32)] + [rows(e) for e in ends[:-1]], axis=0)
    x_end = own_end - b
    x_start = b - own_start
    lhs_lo, lhs_up = [], []
    for g in range(N_SUB - 1):
        cut = (g + 1) * SUB
        lhs_lo.append(q[cut:, :] * jnp.exp(b[cut:, :] - ends[g]))
        lhs_up.append(q[:cut, :] * jnp.exp(ends[g] - b[:cut, :]))
    lhs_lo.append(q * jnp.exp(-x_end))
    lhs_up.append(q * jnp.exp(-x_start))
    k_lo = (k * jnp.exp(x_end)).astype(BF16)
    k_up = (k * jnp.exp(x_start)).astype(BF16)
    zero = jnp.zeros((), BF16)
    blockdiag = lambda kk: jnp.where(head_mask, jnp.concatenate([kk] * GLA_HEADS, axis=0), zero)
    stack = lambda parts: jnp.concatenate(parts, axis=0).astype(BF16)
    return stack(lhs_lo), blockdiag(k_lo), stack(lhs_up), blockdiag(k_up)


_LO_OFF = [sum(CHUNK - (h + 1) * SUB for h in range(g)) for g in range(N_SUB - 1)]
_UP_OFF = [sum((h + 1) * SUB for h in range(g)) for g in range(N_SUB - 1)]
_DIAG_OFF = sum((h + 1) * SUB for h in range(N_SUB - 1))


def _assemble_scores(r_lo, r_up, col_blk, lower_tri):
    out = []
    for i in range(N_SUB):
        d = _DIAG_OFF + i * SUB
        acc = jnp.where(lower_tri, r_lo[d:d + SUB, :], r_up[d:d + SUB, :])
        for j in range(N_SUB):
            if j < i:
                off = _LO_OFF[j] + (i - j - 1) * SUB
                acc = jnp.where(col_blk == j, r_lo[off:off + SUB, :], acc)
            elif j > i:
                off = _UP_OFF[j - 1] + i * SUB
                acc = jnp.where(col_blk == j, r_up[off:off + SUB, :], acc)
        out.append(acc)
    return jnp.concatenate(out, axis=0)


def _trailing_means(aext_ref, sa_ref, sb_ref, rows):
    assert POOL_WINDOWS == (2, 4, 8, 16) and POOL_PAD == 8
    end = POOL_BASE + rows
    lane = lambda g: slice(g * POOL_GROUP_DIM, POOL_WIDTH)
    one = lambda g: slice(g * POOL_GROUP_DIM, (g + 1) * POOL_GROUP_DIM)
    sa_ref[POOL_PAD:end, :] = aext_ref[POOL_PAD:end, :] + aext_ref[POOL_PAD - 1:end - 1, :]
    sb_ref[POOL_PAD:end, lane(1)] = sa_ref[POOL_PAD:end, lane(1)] + sa_ref[POOL_PAD - 2:end - 2, lane(1)]
    sa_ref[POOL_PAD:end, lane(2)] = sb_ref[POOL_PAD:end, lane(2)] + sb_ref[POOL_PAD - 4:end - 4, lane(2)]
    sums = [sa_ref[POOL_BASE:end, one(0)], sb_ref[POOL_BASE:end, one(1)], sa_ref[POOL_BASE:end, one(2)],
            sa_ref[POOL_BASE:end, one(3)] + sa_ref[POOL_BASE - 8:end - 8, one(3)]]
    return [(sums[g] * (1.0 / w) - aext_ref[POOL_BASE:end, one(g)]).astype(BF16)
            for g, w in enumerate(POOL_WINDOWS)]


def _mixer_kernel(x_ref, nmix_ref, wmain_ref, wgate_ref, wg_ref, wpm_ref, pscale_ref, wpo_ref, bg_ref, gnorm_ref,
                  wgo_ref, wout_ref, ameta_ref, st1_ref, h_ref, aext_ref, psum_a_ref, psum_b_ref, st_ref):
    assert N_SUB == 4
    rows = x_ref.shape[0]
    n_chunks = rows // CHUNK
    chunk = lambda arr, c: arr[c * CHUNK:(c + 1) * CHUNK, :]

    @pl.when(pl.program_id(1) == 0)
    def _():
        zero_pad = jnp.zeros((POOL_PAD, POOL_WIDTH), F32)
        aext_ref[0:POOL_PAD, :] = zero_pad
        psum_a_ref[0:POOL_PAD, :] = zero_pad
        psum_b_ref[0:POOL_PAD, :] = zero_pad
        aext_ref[POOL_PAD:POOL_BASE, :] = ameta_ref[...]
        st_ref[...] = st1_ref[...]

    x = x_ref[...]
    xg = x * nmix_ref[...]
    u_raw = xg.astype(BF16)
    row_scale = lax.rsqrt(jnp.mean(x * x, axis=-1, keepdims=True) + EPS)
    u = (xg * row_scale).astype(BF16)
    proj = lambda lo, hi: _dot(u, wmain_ref[:, lo:hi])

    gate_lin = _dot(u_raw, wgate_ref[...]) * row_scale
    zqk = _dot(u_raw, wmain_ref[:, OFF_Q:OFF_V]) * row_scale
    z_a = proj(OFF_A, OFF_Q)
    la = _log_decay(gate_lin, bg_ref)
    v = proj(OFF_V, OFF_R)
    b = [_chunk_cumsum(chunk(la, c)) for c in range(n_chunks)]
    r = proj(OFF_R, N_MAIN)
    q = zqk[:, :GLA_KEY]
    k = zqk[:, GLA_KEY:]

    aext_ref[POOL_BASE:POOL_BASE + rows, :] = z_a
    pooled = _trailing_means(aext_ref, psum_a_ref, psum_b_ref, rows)
    aext_ref[POOL_PAD:POOL_BASE, :] = aext_ref[rows + POOL_PAD:rows + POOL_BASE, :]

    ci = lax.broadcasted_iota(jnp.int32, (SUB, GLA_HEADS * CHUNK), 1) % CHUNK
    col_blk = ci // SUB
    lower_tri = ci % SUB <= lax.broadcasted_iota(jnp.int32, (SUB, GLA_HEADS * CHUNK), 0)
    head_mask = (lax.broadcasted_iota(jnp.int32, (GLA_HEADS * CHUNK, GLA_KEY), 0) // CHUNK
                 == lax.broadcasted_iota(jnp.int32, (GLA_HEADS * CHUNK, GLA_KEY), 1) // GLA_DK)
    operands = [_score_operands(chunk(q, c), chunk(k, c), b[c], head_mask) for c in range(n_chunks)]
    half = D_MODEL // 2
    raw, incr = [], []

    def scores_and_increments(c0, c1):
        for c in range(c0, c1):
            lhs_lo, rhs_lo, lhs_up, rhs_up = operands[c]
            raw.append((_dot_nt(lhs_lo, rhs_lo), _dot_nt(lhs_up, rhs_up)))
            incr.append(_state_increment(chunk(k, c), chunk(v, c), b[c]))

    group = n_chunks // 4
    scores_and_increments(0, group)
    zg_a0 = _dot(u, wg_ref[:, :half])
    scores_and_increments(group, 2 * group)
    zg_a1 = _dot(u, wg_ref[:, half:D_MODEL])
    scores_and_increments(2 * group, 3 * group)
    pm = (_dot(jnp.concatenate(pooled, axis=1), wpm_ref[...]) * pscale_ref[...]).astype(BF16)
    y_a0 = _dot(pm, wpo_ref[:, :half])
    scores_and_increments(3 * group, n_chunks)
    y_a1 = _dot(pm, wpo_ref[:, half:])

    st = st_ref[...]
    o_chunks = []

    def outputs(st, c0, c1):
        for c in range(c0, c1):
            a = _assemble_scores(raw[c][0], raw[c][1], col_blk, lower_tri).astype(BF16)
            vc = chunk(v, c).astype(BF16)
            stb = st.astype(BF16)
            qb = (chunk(q, c) * jnp.exp(b[c])).astype(BF16)
            o_pairs = []
            for p in range(0, GLA_HEADS, 2):
                heads = (p, p + 1)
                v_blk = _block_diag([vc[:, h * GLA_DV:(h + 1) * GLA_DV] for h in heads])
                s_blk = _block_diag([stb[h * GLA_DK:(h + 1) * GLA_DK, :] for h in heads])
                o_pairs.append(_dot(a[:, p * CHUNK:(p + 2) * CHUNK], v_blk)
                               + _dot(qb[:, p * GLA_DK:(p + 2) * GLA_DK], s_blk))
            o_chunks.append(jnp.concatenate(o_pairs, axis=1))
            decay = jnp.exp(b[c][CHUNK - 8:CHUNK, :].T[:, 7:8])
            st = st * decay + incr[c]
        return st

    st = outputs(st, 0, n_chunks)
    st_ref[...] = st
    o = jnp.concatenate(o_chunks, axis=0)
    y_a = jnp.concatenate([y_a0, y_a1], axis=1)
    zg_a = jnp.concatenate([zg_a0, zg_a1], axis=1)

    zg_b0 = _dot(u, wg_ref[:, D_MODEL:D_MODEL + half])
    gnorm = gnorm_ref[...]
    on = []
    for h in range(GLA_HEADS):
        lanes = slice(h * GLA_DV, (h + 1) * GLA_DV)
        rh = r[:, lanes]
        on.append(_rmsnorm(o[:, lanes], gnorm[:, lanes]) * (rh * _twice_sigmoid_of_twice(rh)))
    m_a = _twice_sigmoid_of_twice(zg_a) * y_a
    y_b = _dot(jnp.concatenate(on, axis=1).astype(BF16), wgo_ref[...])
    zg_b1 = _dot(u, wg_ref[:, D_MODEL + half:])

    m = m_a + _twice_sigmoid_of_twice(jnp.concatenate([zg_b0, zg_b1], axis=1)) * y_b
    h_ref[...] = x_ref[...] + _dot(m.astype(BF16), wout_ref[...])


def _mlp_kernel(h_ref, nffn_ref, w1_ref, w2_ref, nfin_ref, o_ref):
    h = h_ref[...]
    c = lax.rsqrt(jnp.mean(h * h, axis=-1, keepdims=True) + EPS)
    hid = jnp.square(jnp.maximum(_dot((h * nffn_ref[...]).astype(BF16), w1_ref[...]), 0.0)).astype(BF16)
    for lo in range(0, h_ref.shape[0], MLP_OUT_ROWS):
        rows = slice(lo, lo + MLP_OUT_ROWS)
        h2 = h_ref[rows, :] + _dot(hid[rows, :], w2_ref[...]) * (c[rows, :] * c[rows, :])
        o_ref[rows, :] = _rmsnorm(h2, nfin_ref[...])


def _resident(shape):
    nd = len(shape)
    return pl.BlockSpec(shape, lambda *_: (0,) * nd, pipeline_mode=pl.Buffered(1))


def kernel(x, meta_tokens, norm_mix, w_in, w_pool_mix, pool_scale, w_pool_out, w_gate_up, b_gate, gla_norm, w_gla_out,
           w_out, norm_ffn, w_ff1, w_ff2, norm_final):
    batch, seq, d_model = x.shape
    assert d_model == D_MODEL and norm_mix.shape[0] == 1 and meta_tokens.shape[0] == N_META
    assert seq % SEQ_TILE == 0 and SEQ_TILE % CHUNK == 0 and (batch * seq) % ROW_TILE == 0

    assert GLA_DK == 64
    wi = w_in[0]
    col_scale = np.ones((1, N_MAIN), np.float32)
    col_scale[:, OFF_Q:OFF_K] = GLA_DK ** -0.5
    col_scale[:, OFF_R:N_MAIN] = 0.5
    w_main = (wi[:, :N_MAIN] * col_scale).astype(BF16)
    w_lr = wi[:, N_MAIN:N_MAIN + GLA_GATE_RANK]
    w_g = (0.5 * wi[:, N_MAIN + GLA_GATE_RANK:]).astype(BF16)
    zero_blk = jnp.zeros((POOL_GROUP_DIM, POOL_GROUP_DIM), w_pool_mix.dtype)
    w_pm = jnp.concatenate(
        [jnp.concatenate([w_pool_mix[0, g] if h == g else zero_blk for h in range(POOL_GROUPS)], axis=1)
         for g in range(POOL_GROUPS)], axis=0).astype(BF16)
    w_po = w_pool_out[0].astype(BF16)
    w_go = w_gla_out[0].astype(BF16)
    w_o = (0.5 * w_out[0]).astype(BF16)
    w_1 = w_ff1[0].astype(BF16)
    w_2 = w_ff2[0].astype(BF16)
    n_fin = norm_final.reshape(1, D_MODEL)

    a_meta, st1, w_gate = pl.pallas_call(
        _meta_kernel,
        out_shape=(jax.ShapeDtypeStruct((N_META, POOL_WIDTH), F32), jax.ShapeDtypeStruct((GLA_KEY, GLA_DV), F32),
                   jax.ShapeDtypeStruct((D_MODEL, GLA_KEY), BF16)),
        name="meta",
    )(meta_tokens, norm_mix, w_main, w_lr, w_gate_up[0], b_gate)

    h1 = pl.pallas_call(
        _mixer_kernel,
        grid=(batch, seq // SEQ_TILE),
        in_specs=[
            pl.BlockSpec((None, SEQ_TILE, D_MODEL), lambda b, j: (b, j, 0)),
            _resident((1, D_MODEL)),
            _resident((D_MODEL, N_MAIN)),
            _resident((D_MODEL, GLA_KEY)),
            _resident((D_MODEL, 2 * D_MODEL)),
            _resident((POOL_WIDTH, POOL_WIDTH)),
            _resident((1, POOL_WIDTH)),
            _resident((POOL_WIDTH, D_MODEL)),
            _resident((1, GLA_KEY)),
            _resident((1, GLA_VAL)),
            _resident((GLA_VAL, D_MODEL)),
            _resident((D_MODEL, D_MODEL)),
            _resident((N_META, POOL_WIDTH)),
            _resident((GLA_KEY, GLA_DV)),
        ],
        out_specs=pl.BlockSpec((None, SEQ_TILE, D_MODEL), lambda b, j: (b, j, 0)),
        out_shape=jax.ShapeDtypeStruct((batch, seq, D_MODEL), F32),
        scratch_shapes=[pltpu.VMEM((POOL_BASE + SEQ_TILE, POOL_WIDTH), F32)] * 3 + [pltpu.VMEM((GLA_KEY, GLA_DV), F32)],
        compiler_params=pltpu.CompilerParams(dimension_semantics=("arbitrary", "arbitrary"),
                                             vmem_limit_bytes=V7X_VMEM_BYTES * 3 // 4),
        name="mixer",
    )(x, norm_mix, w_main, w_gate, w_g, w_pm, pool_scale, w_po, b_gate, gla_norm, w_go, w_o, a_meta, st1)

    n_rows = batch * seq
    out = pl.pallas_call(
        _mlp_kernel,
        grid=(n_rows // ROW_TILE,),
        in_specs=[
            pl.BlockSpec((ROW_TILE, D_MODEL), lambda i: (i, 0)),
            _resident((1, D_MODEL)),
            _resident((D_MODEL, D_FF)),
            _resident((D_FF, D_MODEL)),
            _resident((1, D_MODEL)),
        ],
        out_specs=pl.BlockSpec((ROW_TILE, D_MODEL), lambda i: (i, 0)),
        out_shape=jax.ShapeDtypeStruct((n_rows, D_MODEL), F32),
        compiler_params=pltpu.CompilerParams(dimension_semantics=("arbitrary",),
                                             vmem_limit_bytes=V7X_VMEM_BYTES * 3 // 4),
        name="mlp",
    )(h1.reshape(n_rows, D_MODEL), norm_ffn, w_1, w_2, n_fin)
    return out.reshape(batch, seq, D_MODEL)
```

```python
import jax
import jax.numpy as jnp
import numpy as np
from jax import lax
from jax.experimental import pallas as pl
from jax.experimental.pallas import tpu as pltpu

F32 = jnp.float32
BF16 = jnp.bfloat16

D_MODEL = 1024
N_META = 16
CHUNK = 64
EPS = 1e-6
POOL_WIDTH = 512
POOL_GROUPS = 4
POOL_GROUP_DIM = POOL_WIDTH // POOL_GROUPS
POOL_WINDOWS = (2, 4, 8, 16)
MAX_WINDOW = max(POOL_WINDOWS)
GLA_HEADS = 4
GLA_DK = 64
GLA_DV = 128
GLA_KEY = GLA_HEADS * GLA_DK
GLA_VAL = GLA_HEADS * GLA_DV
GLA_GATE_RANK = 16
GLA_TAU = 16.0
D_FF = 4 * D_MODEL

V7X_VMEM_BYTES = 64 * 1024 * 1024

OFF_A = 0
OFF_Q = OFF_A + POOL_WIDTH
OFF_K = OFF_Q + GLA_KEY
OFF_V = OFF_K + GLA_KEY
OFF_R = OFF_V + GLA_VAL
N_MAIN = OFF_R + GLA_VAL

SEQ_TILE = 1024
ROW_TILE = 1024
MLP_OUT_ROWS = 256
SUB = 16
N_SUB = CHUNK // SUB
POOL_PAD = 8
POOL_BASE = POOL_PAD + MAX_WINDOW


def _rmsnorm(x, g):
    return x * lax.rsqrt(jnp.mean(x * x, axis=-1, keepdims=True) + EPS) * g


def _twice_sigmoid_of_twice(x):
    return 1.0 + jnp.tanh(x)


def _dot(a, b):
    return jnp.dot(a, b, preferred_element_type=F32)


def _dot_nt(a, b):
    return lax.dot_general(a, b, (((1,), (1,)), ((), ())), preferred_element_type=F32)


def _chunk_cumsum(la):
    c = la.shape[0]
    tri = (lax.broadcasted_iota(jnp.int32, (c, c), 1) <= lax.broadcasted_iota(jnp.int32, (c, c), 0)).astype(BF16)
    return _dot(tri, la.astype(BF16))


def _log_decay(gate_lin, bg_ref):
    return jax.nn.log_sigmoid(gate_lin + bg_ref[...]) * (1.0 / GLA_TAU)


def _state_increment(k, v, b):
    kb_t = (k * jnp.exp(b[-1:, :] - b)).T.astype(BF16)
    vb = v.astype(BF16)
    out = []
    for p in range(GLA_HEADS // 2):
        pair = _dot(kb_t[2 * p * GLA_DK:2 * (p + 1) * GLA_DK, :], vb[:, 2 * p * GLA_DV:2 * (p + 1) * GLA_DV])
        out += [pair[:GLA_DK, :GLA_DV], pair[GLA_DK:, GLA_DV:]]
    return jnp.concatenate(out, axis=0)


def _block_diag(blocks):
    n = len(blocks)
    zero = jnp.zeros(blocks[0].shape, blocks[0].dtype)
    return jnp.concatenate(
        [jnp.concatenate([blocks[i] if j == i else zero for j in range(n)], axis=1) for i in range(n)], axis=0)


def _meta_kernel(meta_ref, nmix_ref, wmain_t_ref, wlr_t_ref, wup_t_ref, bg_ref, a_ref, st_ref, wgate_t_ref):
    w_gate_t = jnp.dot(wup_t_ref[...], wlr_t_ref[...], preferred_element_type=F32,
                       precision=lax.Precision.HIGHEST).astype(BF16)
    wgate_t_ref[...] = w_gate_t
    u = _rmsnorm(meta_ref[...], nmix_ref[...]).astype(BF16)
    z = _dot_nt(u, wmain_t_ref[...])
    a_ref[...] = z[:, OFF_A:OFF_A + POOL_WIDTH]
    la = _log_decay(_dot_nt(u, w_gate_t), bg_ref)
    st_ref[...] = _state_increment(z[:, OFF_K:OFF_K + GLA_KEY], z[:, OFF_V:OFF_V + GLA_VAL], _chunk_cumsum(la))


def _score_operands(q, k, b, head_mask):
    ends = [b[(i + 1) * SUB - 1:(i + 1) * SUB, :] for i in range(N_SUB)]
    rows = lambda e: jnp.broadcast_to(e, (SUB, GLA_KEY))
    own_end = jnp.concatenate([rows(e) for e in ends], axis=0)
    own_start = jnp.concatenate([jnp.zeros((SUB, GLA_KEY), F32)] + [rows(e) for e in ends[:-1]], axis=0)
    x_end = own_end - b
    x_start = b - own_start
    lhs_lo, lhs_up = [], []
    for g in range(N_SUB - 1):
        cut = (g + 1) * SUB
        lhs_lo.append(q[cut:, :] * jnp.exp(b[cut:, :] - ends[g]))
        lhs_up.append(q[:cut, :] * jnp.exp(ends[g] - b[:cut, :]))
    lhs_lo.append(q * jnp.exp(-x_end))
    lhs_up.append(q * jnp.exp(-x_start))
    k_lo = (k * jnp.exp(x_end)).astype(BF16)
    k_up = (k * jnp.exp(x_start)).astype(BF16)
    zero = jnp.zeros((), BF16)
    blockdiag = lambda kk: jnp.where(head_mask, jnp.concatenate([kk] * GLA_HEADS, axis=0), zero)
    stack = lambda parts: jnp.concatenate(parts, axis=0).astype(BF16)
    return stack(lhs_lo), blockdiag(k_lo), stack(lhs_up), blockdiag(k_up)


_LO_OFF = [sum(CHUNK - (h + 1) * SUB for h in range(g)) for g in range(N_SUB - 1)]
_UP_OFF = [sum((h + 1) * SUB for h in range(g)) for g in range(N_SUB - 1)]
_DIAG_OFF = sum((h + 1) * SUB for h in range(N_SUB - 1))


def _assemble_scores(r_lo, r_up, col_blk, lower_tri):
    out = []
    for i in range(N_SUB):
        d = _DIAG_OFF + i * SUB
        acc = jnp.where(lower_tri, r_lo[d:d + SUB, :], r_up[d:d + SUB, :])
        for j in range(N_SUB):
            if j < i:
                off = _LO_OFF[j] + (i - j - 1) * SUB
                acc = jnp.where(col_blk == j, r_lo[off:off + SUB, :], acc)
            elif j > i:
                off = _UP_OFF[j - 1] + i * SUB
                acc = jnp.where(col_blk == j, r_up[off:off + SUB, :], acc)
        out.append(acc)
    return jnp.concatenate(out, axis=0)


def _trailing_means(aext_ref, sa_ref, sb_ref, rows):
    assert POOL_WINDOWS == (2, 4, 8, 16) and POOL_PAD == 8
    end = POOL_BASE + rows
    lane = lambda g: slice(g * POOL_GROUP_DIM, POOL_WIDTH)
    one = lambda g: slice(g * POOL_GROUP_DIM, (g + 1) * POOL_GROUP_DIM)
    sa_ref[POOL_PAD:end, :] = aext_ref[POOL_PAD:end, :] + aext_ref[POOL_PAD - 1:end - 1, :]
    sb_ref[POOL_PAD:end, lane(1)] = sa_ref[POOL_PAD:end, lane(1)] + sa_ref[POOL_PAD - 2:end - 2, lane(1)]
    sa_ref[POOL_PAD:end, lane(2)] = sb_ref[POOL_PAD:end, lane(2)] + sb_ref[POOL_PAD - 4:end - 4, lane(2)]
    sums = [sa_ref[POOL_BASE:end, one(0)], sb_ref[POOL_BASE:end, one(1)], sa_ref[POOL_BASE:end, one(2)],
            sa_ref[POOL_BASE:end, one(3)] + sa_ref[POOL_BASE - 8:end - 8, one(3)]]
    return [(sums[g] * (1.0 / w) - aext_ref[POOL_BASE:end, one(g)]).astype(BF16)
            for g, w in enumerate(POOL_WINDOWS)]


def _mixer_kernel(x_ref, nmix_ref, wmain_t_ref, wgate_t_ref, wg_t_ref, wpm_ref, pscale_ref, wpo_ref, bg_ref, gnorm_ref,
                  wgo_ref, wout_ref, ameta_ref, st1_ref, h_ref, aext_ref, psum_a_ref, psum_b_ref, st_ref):
    assert N_SUB == 4
    rows = x_ref.shape[0]
    n_chunks = rows // CHUNK
    chunk = lambda arr, c: arr[c * CHUNK:(c + 1) * CHUNK, :]

    @pl.when(pl.program_id(1) == 0)
    def _():
        zero_pad = jnp.zeros((POOL_PAD, POOL_WIDTH), F32)
        aext_ref[0:POOL_PAD, :] = zero_pad
        psum_a_ref[0:POOL_PAD, :] = zero_pad
        psum_b_ref[0:POOL_PAD, :] = zero_pad
        aext_ref[POOL_PAD:POOL_BASE, :] = ameta_ref[...]
        st_ref[...] = st1_ref[...]

    x = x_ref[...]
    xg = x * nmix_ref[...]
    u_raw = xg.astype(BF16)
    row_scale = lax.rsqrt(jnp.mean(x * x, axis=-1, keepdims=True) + EPS)
    u = (xg * row_scale).astype(BF16)
    proj = lambda lo, hi: _dot_nt(u, wmain_t_ref[lo:hi, :])

    gate_lin = _dot_nt(u_raw, wgate_t_ref[...]) * row_scale
    zqk = _dot_nt(u_raw, wmain_t_ref[OFF_Q:OFF_V, :]) * row_scale
    z_a = proj(OFF_A, OFF_Q)
    la = _log_decay(gate_lin, bg_ref)
    v = proj(OFF_V, OFF_R)
    b = [_chunk_cumsum(chunk(la, c)) for c in range(n_chunks)]
    r = proj(OFF_R, N_MAIN)
    q = zqk[:, :GLA_KEY]
    k = zqk[:, GLA_KEY:]

    aext_ref[POOL_BASE:POOL_BASE + rows, :] = z_a
    pooled = _trailing_means(aext_ref, psum_a_ref, psum_b_ref, rows)
    aext_ref[POOL_PAD:POOL_BASE, :] = aext_ref[rows + POOL_PAD:rows + POOL_BASE, :]

    ci = lax.broadcasted_iota(jnp.int32, (SUB, GLA_HEADS * CHUNK), 1) % CHUNK
    col_blk = ci // SUB
    lower_tri = ci % SUB <= lax.broadcasted_iota(jnp.int32, (SUB, GLA_HEADS * CHUNK), 0)
    head_mask = (lax.broadcasted_iota(jnp.int32, (GLA_HEADS * CHUNK, GLA_KEY), 0) // CHUNK
                 == lax.broadcasted_iota(jnp.int32, (GLA_HEADS * CHUNK, GLA_KEY), 1) // GLA_DK)
    operands = [_score_operands(chunk(q, c), chunk(k, c), b[c], head_mask) for c in range(n_chunks)]
    half = D_MODEL // 2
    raw, incr = [], []

    def scores_and_increments(c0, c1):
        for c in range(c0, c1):
            lhs_lo, rhs_lo, lhs_up, rhs_up = operands[c]
            raw.append((_dot_nt(lhs_lo, rhs_lo), _dot_nt(lhs_up, rhs_up)))
            incr.append(_state_increment(chunk(k, c), chunk(v, c), b[c]))

    group = n_chunks // 4
    scores_and_increments(0, group)
    zg_a0 = _dot_nt(u, wg_t_ref[:half, :])
    scores_and_increments(group, 2 * group)
    zg_a1 = _dot_nt(u, wg_t_ref[half:D_MODEL, :])
    scores_and_increments(2 * group, 3 * group)
    pm = (_dot(jnp.concatenate(pooled, axis=1), wpm_ref[...]) * pscale_ref[...]).astype(BF16)
    y_a0 = _dot(pm, wpo_ref[:, :half])
    scores_and_increments(3 * group, n_chunks)
    y_a1 = _dot(pm, wpo_ref[:, half:])

    st = st_ref[...]
    o_chunks = []

    def outputs(st, c0, c1):
        for c in range(c0, c1):
            a = _assemble_scores(raw[c][0], raw[c][1], col_blk, lower_tri).astype(BF16)
            vc = chunk(v, c).astype(BF16)
            stb = st.astype(BF16)
            qb = (chunk(q, c) * jnp.exp(b[c])).astype(BF16)
            o_pairs = []
            for p in range(0, GLA_HEADS, 2):
                heads = (p, p + 1)
                v_blk = _block_diag([vc[:, h * GLA_DV:(h + 1) * GLA_DV] for h in heads])
                s_blk = _block_diag([stb[h * GLA_DK:(h + 1) * GLA_DK, :] for h in heads])
                o_pairs.append(_dot(a[:, p * CHUNK:(p + 2) * CHUNK], v_blk)
                               + _dot(qb[:, p * GLA_DK:(p + 2) * GLA_DK], s_blk))
            o_chunks.append(jnp.concatenate(o_pairs, axis=1))
            decay = jnp.exp(b[c][CHUNK - 8:CHUNK, :].T[:, 7:8])
            st = st * decay + incr[c]
        return st

    st = outputs(st, 0, n_chunks)
    st_ref[...] = st
    o = jnp.concatenate(o_chunks, axis=0)
    y_a = jnp.concatenate([y_a0, y_a1], axis=1)
    zg_a = jnp.concatenate([zg_a0, zg_a1], axis=1)

    zg_b0 = _dot_nt(u, wg_t_ref[D_MODEL:D_MODEL + half, :])
    gnorm = gnorm_ref[...]
    on = []
    for h in range(GLA_HEADS):
        lanes = slice(h * GLA_DV, (h + 1) * GLA_DV)
        rh = r[:, lanes]
        on.append(_rmsnorm(o[:, lanes], gnorm[:, lanes]) * (rh * _twice_sigmoid_of_twice(rh)))
    m_a = _twice_sigmoid_of_twice(zg_a) * y_a
    y_b = _dot(jnp.concatenate(on, axis=1).astype(BF16), wgo_ref[...])
    zg_b1 = _dot_nt(u, wg_t_ref[D_MODEL + half:, :])

    m = m_a + _twice_sigmoid_of_twice(jnp.concatenate([zg_b0, zg_b1], axis=1)) * y_b
    h_ref[...] = x_ref[...] + _dot(m.astype(BF16), wout_ref[...])


def _mlp_kernel(h_ref, nffn_ref, w1_ref, w2_ref, nfin_ref, o_ref):
    h = h_ref[...]
    c = lax.rsqrt(jnp.mean(h * h, axis=-1, keepdims=True) + EPS)
    hid = jnp.square(jnp.maximum(_dot((h * nffn_ref[...]).astype(BF16), w1_ref[...]), 0.0)).astype(BF16)
    for lo in range(0, h_ref.shape[0], MLP_OUT_ROWS):
        rows = slice(lo, lo + MLP_OUT_ROWS)
        h2 = h_ref[rows, :] + _dot(hid[rows, :], w2_ref[...]) * (c[rows, :] * c[rows, :])
        o_ref[rows, :] = _rmsnorm(h2, nfin_ref[...])


def _resident(shape):
    nd = len(shape)
    return pl.BlockSpec(shape, lambda *_: (0,) * nd, pipeline_mode=pl.Buffered(1))


def kernel(x, meta_tokens, norm_mix, w_in, w_pool_mix, pool_scale, w_pool_out, w_gate_up, b_gate, gla_norm, w_gla_out,
           w_out, norm_ffn, w_ff1, w_ff2, norm_final):
    batch, seq, d_model = x.shape
    assert d_model == D_MODEL and norm_mix.shape[0] == 1 and meta_tokens.shape[0] == N_META
    assert seq % SEQ_TILE == 0 and SEQ_TILE % CHUNK == 0 and (batch * seq) % ROW_TILE == 0

    assert GLA_DK == 64
    wt = jnp.transpose(w_in[0])
    col_scale = np.ones((N_MAIN, 1), np.float32)
    col_scale[OFF_Q:OFF_K] = GLA_DK ** -0.5
    col_scale[OFF_R:N_MAIN] = 0.5
    w_main_t = (wt[:N_MAIN] * col_scale).astype(BF16)
    w_lr_t = wt[N_MAIN:N_MAIN + GLA_GATE_RANK]
    w_g_t = (0.5 * wt[N_MAIN + GLA_GATE_RANK:]).astype(BF16)
    w_up_t = jnp.transpose(w_gate_up[0])
    zero_blk = jnp.zeros((POOL_GROUP_DIM, POOL_GROUP_DIM), w_pool_mix.dtype)
    w_pm = jnp.concatenate(
        [jnp.concatenate([w_pool_mix[0, g] if h == g else zero_blk for h in range(POOL_GROUPS)], axis=1)
         for g in range(POOL_GROUPS)], axis=0).astype(BF16)
    w_po = w_pool_out[0].astype(BF16)
    w_go = w_gla_out[0].astype(BF16)
    w_o = (0.5 * w_out[0]).astype(BF16)
    w_1 = w_ff1[0].astype(BF16)
    w_2 = w_ff2[0].astype(BF16)
    n_fin = norm_final.reshape(1, D_MODEL)

    a_meta, st1, w_gate_t = pl.pallas_call(
        _meta_kernel,
        out_shape=(jax.ShapeDtypeStruct((N_META, POOL_WIDTH), F32), jax.ShapeDtypeStruct((GLA_KEY, GLA_DV), F32),
                   jax.ShapeDtypeStruct((GLA_KEY, D_MODEL), BF16)),
        name="meta",
    )(meta_tokens, norm_mix, w_main_t, w_lr_t, w_up_t, b_gate)

    h1 = pl.pallas_call(
        _mixer_kernel,
        grid=(batch, seq // SEQ_TILE),
        in_specs=[
            pl.BlockSpec((None, SEQ_TILE, D_MODEL), lambda b, j: (b, j, 0)),
            _resident((1, D_MODEL)),
            _resident((N_MAIN, D_MODEL)),
            _resident((GLA_KEY, D_MODEL)),
            _resident((2 * D_MODEL, D_MODEL)),
            _resident((POOL_WIDTH, POOL_WIDTH)),
            _resident((1, POOL_WIDTH)),
            _resident((POOL_WIDTH, D_MODEL)),
            _resident((1, GLA_KEY)),
            _resident((1, GLA_VAL)),
            _resident((GLA_VAL, D_MODEL)),
            _resident((D_MODEL, D_MODEL)),
            _resident((N_META, POOL_WIDTH)),
            _resident((GLA_KEY, GLA_DV)),
        ],
        out_specs=pl.BlockSpec((None, SEQ_TILE, D_MODEL), lambda b, j: (b, j, 0)),
        out_shape=jax.ShapeDtypeStruct((batch, seq, D_MODEL), F32),
        scratch_shapes=[pltpu.VMEM((POOL_BASE + SEQ_TILE, POOL_WIDTH), F32)] * 3 + [pltpu.VMEM((GLA_KEY, GLA_DV), F32)],
        compiler_params=pltpu.CompilerParams(dimension_semantics=("arbitrary", "arbitrary"),
                                             vmem_limit_bytes=V7X_VMEM_BYTES * 3 // 4),
        name="mixer",
    )(x, norm_mix, w_main_t, w_gate_t, w_g_t, w_pm, pool_scale, w_po, b_gate, gla_norm, w_go, w_o, a_meta, st1)

    n_rows = batch * seq
    out = pl.pallas_call(
        _mlp_kernel,
        grid=(n_rows // ROW_TILE,),
        in_specs=[
            pl.BlockSpec((ROW_TILE, D_MODEL), lambda i: (i, 0)),
            _resident((1, D_MODEL)),
            _resident((D_MODEL, D_FF)),
            _resident((D_FF, D_MODEL)),
            _resident((1, D_MODEL)),
        ],
        out_specs=pl.BlockSpec((ROW_TILE, D_MODEL), lambda i: (i, 0)),
        out_shape=jax.ShapeDtypeStruct((n_rows, D_MODEL), F32),
        compiler_params=pltpu.CompilerParams(dimension_semantics=("arbitrary",),
                                             vmem_limit_bytes=V7X_VMEM_BYTES * 3 // 4),
        name="mlp",
    )(h1.reshape(n_rows, D_MODEL), norm_ffn, w_1, w_2, n_fin)
    return out.reshape(batch, seq, D_MODEL)
```

```python
import jax
import jax.numpy as jnp
import numpy as np
from jax import lax
from jax.experimental import pallas as pl
from jax.experimental.pallas import tpu as pltpu

F32 = jnp.float32
BF16 = jnp.bfloat16

D_MODEL = 1024
N_META = 16
CHUNK = 64
EPS = 1e-6
POOL_WIDTH = 512
POOL_GROUPS = 4
POOL_GROUP_DIM = POOL_WIDTH // POOL_GROUPS
POOL_WINDOWS = (2, 4, 8, 16)
MAX_WINDOW = max(POOL_WINDOWS)
GLA_HEADS = 4
GLA_DK = 64
GLA_DV = 128
GLA_KEY = GLA_HEADS * GLA_DK
GLA_VAL = GLA_HEADS * GLA_DV
GLA_GATE_RANK = 16
GLA_TAU = 16.0
D_FF = 4 * D_MODEL

V7X_VMEM_BYTES = 64 * 1024 * 1024
VMEM_LIMIT_BYTES = V7X_VMEM_BYTES * 3 // 4

OFF_A = 0
OFF_Q = OFF_A + POOL_WIDTH
OFF_K = OFF_Q + GLA_KEY
OFF_V = OFF_K + GLA_KEY
OFF_R = OFF_V + GLA_VAL
N_MAIN = OFF_R + GLA_VAL

SEQ_TILE = 1024
ROW_TILE = 1024
MLP_OUT_ROWS = 256
SUB = 16
N_SUB = CHUNK // SUB
POOL_PAD = 8
POOL_BASE = POOL_PAD + MAX_WINDOW


def _rmsnorm(x, g):
    return x * lax.rsqrt(jnp.mean(x * x, axis=-1, keepdims=True) + EPS) * g


def _twice_sigmoid_of_twice(x):
    return 1.0 + jnp.tanh(x)


def _dot(a, b):
    return jnp.dot(a, b, preferred_element_type=F32)


def _dot_nt(a, b):
    return lax.dot_general(a, b, (((1,), (1,)), ((), ())), preferred_element_type=F32)


def _chunk_cumsum(la):
    c = la.shape[0]
    tri = (lax.broadcasted_iota(jnp.int32, (c, c), 1) <= lax.broadcasted_iota(jnp.int32, (c, c), 0)).astype(BF16)
    return _dot(tri, la.astype(BF16))


def _log_decay(gate_lin, bg_ref):
    return jax.nn.log_sigmoid(gate_lin + bg_ref[...]) * (1.0 / GLA_TAU)


def _state_increment(k, v, b):
    kb_t = (k * jnp.exp(b[-1:, :] - b)).T.astype(BF16)
    vb = v.astype(BF16)
    out = []
    for p in range(GLA_HEADS // 2):
        pair = _dot(kb_t[2 * p * GLA_DK:2 * (p + 1) * GLA_DK, :], vb[:, 2 * p * GLA_DV:2 * (p + 1) * GLA_DV])
        out += [pair[:GLA_DK, :GLA_DV], pair[GLA_DK:, GLA_DV:]]
    return jnp.concatenate(out, axis=0)


def _block_diag(blocks):
    n = len(blocks)
    zero = jnp.zeros(blocks[0].shape, blocks[0].dtype)
    return jnp.concatenate(
        [jnp.concatenate([blocks[i] if j == i else zero for j in range(n)], axis=1) for i in range(n)], axis=0)


def _meta_kernel(meta_ref, nmix_ref, wmain_t_ref, wlr_t_ref, wup_t_ref, bg_ref, a_ref, st_ref, wgate_t_ref):
    w_gate_t = jnp.dot(wup_t_ref[...], wlr_t_ref[...], preferred_element_type=F32,
                       precision=lax.Precision.HIGHEST).astype(BF16)
    wgate_t_ref[...] = w_gate_t
    u = _rmsnorm(meta_ref[...], nmix_ref[...]).astype(BF16)
    z = _dot_nt(u, wmain_t_ref[...])
    a_ref[...] = z[:, OFF_A:OFF_A + POOL_WIDTH]
    la = _log_decay(_dot_nt(u, w_gate_t), bg_ref)
    st_ref[...] = _state_increment(z[:, OFF_K:OFF_K + GLA_KEY], z[:, OFF_V:OFF_V + GLA_VAL], _chunk_cumsum(la))


def _score_operands(q, k, b):
    ends = [b[(i + 1) * SUB - 1:(i + 1) * SUB, :] for i in range(N_SUB)]
    rows = lambda e: jnp.broadcast_to(e, (SUB, GLA_KEY))
    own_end = jnp.concatenate([rows(e) for e in ends], axis=0)
    own_start = jnp.concatenate([jnp.zeros((SUB, GLA_KEY), F32)] + [rows(e) for e in ends[:-1]], axis=0)
    x_end = own_end - b
    x_start = b - own_start
    lhs_lo, lhs_up = [], []
    for g in range(N_SUB - 1):
        cut = (g + 1) * SUB
        lhs_lo.append(q[cut:, :] * jnp.exp(b[cut:, :] - ends[g]))
        lhs_up.append(q[:cut, :] * jnp.exp(ends[g] - b[:cut, :]))
    lhs_lo.append(q * jnp.exp(-x_end))
    lhs_up.append(q * jnp.exp(-x_start))

    def blockdiag_t(kk):
        kt = kk.T.astype(BF16)
        return _block_diag([kt[h * GLA_DK:(h + 1) * GLA_DK, :] for h in range(GLA_HEADS)])

    stack = lambda parts: jnp.concatenate(parts, axis=0).astype(BF16)
    return stack(lhs_lo), blockdiag_t(k * jnp.exp(x_end)), stack(lhs_up), blockdiag_t(k * jnp.exp(x_start))


_LO_OFF = [sum(CHUNK - (h + 1) * SUB for h in range(g)) for g in range(N_SUB - 1)]
_UP_OFF = [sum((h + 1) * SUB for h in range(g)) for g in range(N_SUB - 1)]
_DIAG_OFF = sum((h + 1) * SUB for h in range(N_SUB - 1))


def _assemble_scores(r_lo, r_up, col_blk, lower_tri):
    out = []
    for i in range(N_SUB):
        d = _DIAG_OFF + i * SUB
        acc = jnp.where(lower_tri, r_lo[d:d + SUB, :], r_up[d:d + SUB, :])
        for j in range(N_SUB):
            if j < i:
                off = _LO_OFF[j] + (i - j - 1) * SUB
                acc = jnp.where(col_blk == j, r_lo[off:off + SUB, :], acc)
            elif j > i:
                off = _UP_OFF[j - 1] + i * SUB
                acc = jnp.where(col_blk == j, r_up[off:off + SUB, :], acc)
        out.append(acc)
    return jnp.concatenate(out, axis=0)


def _trailing_means(aext_ref, sa_ref, sb_ref, rows):
    assert POOL_WINDOWS == (2, 4, 8, 16) and POOL_PAD == 8
    end = POOL_BASE + rows
    lane = lambda g: slice(g * POOL_GROUP_DIM, POOL_WIDTH)
    one = lambda g: slice(g * POOL_GROUP_DIM, (g + 1) * POOL_GROUP_DIM)
    sa_ref[POOL_PAD:end, :] = aext_ref[POOL_PAD:end, :] + aext_ref[POOL_PAD - 1:end - 1, :]
    sb_ref[POOL_PAD:end, lane(1)] = sa_ref[POOL_PAD:end, lane(1)] + sa_ref[POOL_PAD - 2:end - 2, lane(1)]
    sa_ref[POOL_PAD:end, lane(2)] = sb_ref[POOL_PAD:end, lane(2)] + sb_ref[POOL_PAD - 4:end - 4, lane(2)]
    sums = [sa_ref[POOL_BASE:end, one(0)], sb_ref[POOL_BASE:end, one(1)], sa_ref[POOL_BASE:end, one(2)],
            sa_ref[POOL_BASE:end, one(3)] + sa_ref[POOL_BASE - 8:end - 8, one(3)]]
    return [(sums[g] * (1.0 / w) - aext_ref[POOL_BASE:end, one(g)]).astype(BF16)
            for g, w in enumerate(POOL_WINDOWS)]


def _mixer_kernel(x_ref, nmix_ref, wmain_t_ref, wgate_t_ref, wg_t_ref, wpm_ref, pscale_ref, wpo_ref, bg_ref, gnorm_ref,
                  wgo_ref, wout_ref, ameta_ref, st1_ref, h_ref, aext_ref, psum_a_ref, psum_b_ref, st_ref):
    assert N_SUB == 4
    rows = x_ref.shape[0]
    n_chunks = rows // CHUNK
    chunk = lambda arr, c: arr[c * CHUNK:(c + 1) * CHUNK, :]

    @pl.when(pl.program_id(1) == 0)
    def _():
        zero_pad = jnp.zeros((POOL_PAD, POOL_WIDTH), F32)
        aext_ref[0:POOL_PAD, :] = zero_pad
        psum_a_ref[0:POOL_PAD, :] = zero_pad
        psum_b_ref[0:POOL_PAD, :] = zero_pad
        aext_ref[POOL_PAD:POOL_BASE, :] = ameta_ref[...]
        st_ref[...] = st1_ref[...]

    x = x_ref[...]
    xg = x * nmix_ref[...]
    u_raw = xg.astype(BF16)
    row_scale = lax.rsqrt(jnp.mean(x * x, axis=-1, keepdims=True) + EPS)
    u = (xg * row_scale).astype(BF16)
    proj = lambda lo, hi: _dot_nt(u, wmain_t_ref[lo:hi, :])

    gate_lin = _dot_nt(u_raw, wgate_t_ref[...]) * row_scale
    zqk = _dot_nt(u_raw, wmain_t_ref[OFF_Q:OFF_V, :]) * row_scale
    z_a = proj(OFF_A, OFF_Q)
    la = _log_decay(gate_lin, bg_ref)
    v = proj(OFF_V, OFF_R)
    b = [_chunk_cumsum(chunk(la, c)) for c in range(n_chunks)]
    r = proj(OFF_R, N_MAIN)
    q = zqk[:, :GLA_KEY]
    k = zqk[:, GLA_KEY:]

    aext_ref[POOL_BASE:POOL_BASE + rows, :] = z_a
    pooled = _trailing_means(aext_ref, psum_a_ref, psum_b_ref, rows)
    aext_ref[POOL_PAD:POOL_BASE, :] = aext_ref[rows + POOL_PAD:rows + POOL_BASE, :]

    ci = lax.broadcasted_iota(jnp.int32, (SUB, GLA_HEADS * CHUNK), 1) % CHUNK
    col_blk = ci // SUB
    lower_tri = ci % SUB <= lax.broadcasted_iota(jnp.int32, (SUB, GLA_HEADS * CHUNK), 0)
    operands = [_score_operands(chunk(q, c), chunk(k, c), b[c]) for c in range(n_chunks)]
    half = D_MODEL // 2
    raw, incr = [], []

    def scores_and_increments(c0, c1):
        for c in range(c0, c1):
            lhs_lo, rhs_lo, lhs_up, rhs_up = operands[c]
            raw.append((_dot(lhs_lo, rhs_lo), _dot(lhs_up, rhs_up)))
            incr.append(_state_increment(chunk(k, c), chunk(v, c), b[c]))

    group = n_chunks // 4
    scores_and_increments(0, group)
    zg_a0 = _dot_nt(u, wg_t_ref[:half, :])
    scores_and_increments(group, 2 * group)
    zg_a1 = _dot_nt(u, wg_t_ref[half:D_MODEL, :])
    scores_and_increments(2 * group, 3 * group)
    pm = jnp.concatenate([_dot(jnp.concatenate(pooled[2 * p:2 * p + 2], axis=1), wpm_ref[p])
                          for p in range(POOL_GROUPS // 2)], axis=1)
    pm = (pm * pscale_ref[...]).astype(BF16)
    y_a0 = _dot(pm, wpo_ref[:, :half])
    scores_and_increments(3 * group, n_chunks)
    y_a1 = _dot(pm, wpo_ref[:, half:])

    st = st_ref[...]
    o_chunks = []

    def outputs(st, c0, c1):
        for c in range(c0, c1):
            a = _assemble_scores(raw[c][0], raw[c][1], col_blk, lower_tri).astype(BF16)
            vc = chunk(v, c).astype(BF16)
            stb = st.astype(BF16)
            qb = (chunk(q, c) * jnp.exp(b[c])).astype(BF16)
            o_pairs = []
            for p in range(0, GLA_HEADS, 2):
                heads = (p, p + 1)
                v_blk = _block_diag([vc[:, h * GLA_DV:(h + 1) * GLA_DV] for h in heads])
                s_blk = _block_diag([stb[h * GLA_DK:(h + 1) * GLA_DK, :] for h in heads])
                o_pairs.append(_dot(a[:, p * CHUNK:(p + 2) * CHUNK], v_blk)
                               + _dot(qb[:, p * GLA_DK:(p + 2) * GLA_DK], s_blk))
            o_chunks.append(jnp.concatenate(o_pairs, axis=1))
            decay = jnp.exp(b[c][CHUNK - 8:CHUNK, :].T[:, 7:8])
            st = st * decay + incr[c]
        return st

    st = outputs(st, 0, n_chunks)
    st_ref[...] = st
    o = jnp.concatenate(o_chunks, axis=0)
    y_a = jnp.concatenate([y_a0, y_a1], axis=1)
    zg_a = jnp.concatenate([zg_a0, zg_a1], axis=1)

    zg_b0 = _dot_nt(u, wg_t_ref[D_MODEL:D_MODEL + half, :])
    gnorm = gnorm_ref[...]
    on = []
    for h in range(GLA_HEADS):
        lanes = slice(h * GLA_DV, (h + 1) * GLA_DV)
        rh = r[:, lanes]
        on.append(_rmsnorm(o[:, lanes], gnorm[:, lanes]) * (rh * _twice_sigmoid_of_twice(rh)))
    m_a = _twice_sigmoid_of_twice(zg_a) * y_a
    y_b = _dot(jnp.concatenate(on, axis=1).astype(BF16), wgo_ref[...])
    zg_b1 = _dot_nt(u, wg_t_ref[D_MODEL + half:, :])

    m = m_a + _twice_sigmoid_of_twice(jnp.concatenate([zg_b0, zg_b1], axis=1)) * y_b
    h_ref[...] = x_ref[...] + _dot(m.astype(BF16), wout_ref[...])


def _mlp_kernel(h_ref, nffn_ref, w1_ref, w2_ref, nfin_ref, o_ref):
    h = h_ref[...]
    c = lax.rsqrt(jnp.mean(h * h, axis=-1, keepdims=True) + EPS)
    hid = jnp.square(jnp.maximum(_dot((h * nffn_ref[...]).astype(BF16), w1_ref[...]), 0.0)).astype(BF16)
    for lo in range(0, h_ref.shape[0], MLP_OUT_ROWS):
        rows = slice(lo, lo + MLP_OUT_ROWS)
        h2 = h_ref[rows, :] + _dot(hid[rows, :], w2_ref[...]) * (c[rows, :] * c[rows, :])
        o_ref[rows, :] = _rmsnorm(h2, nfin_ref[...])


def _resident(shape):
    nd = len(shape)
    return pl.BlockSpec(shape, lambda *_: (0,) * nd, pipeline_mode=pl.Buffered(1))


def kernel(x, meta_tokens, norm_mix, w_in, w_pool_mix, pool_scale, w_pool_out, w_gate_up, b_gate, gla_norm, w_gla_out,
           w_out, norm_ffn, w_ff1, w_ff2, norm_final):
    batch, seq, d_model = x.shape
    assert d_model == D_MODEL and norm_mix.shape[0] == 1 and meta_tokens.shape[0] == N_META
    assert seq % SEQ_TILE == 0 and SEQ_TILE % CHUNK == 0 and (batch * seq) % ROW_TILE == 0

    assert GLA_DK == 64
    wt = jnp.transpose(w_in[0])
    col_scale = np.ones((N_MAIN, 1), np.float32)
    col_scale[OFF_Q:OFF_K] = GLA_DK ** -0.5
    col_scale[OFF_R:N_MAIN] = 0.5
    w_main_t = (wt[:N_MAIN] * col_scale).astype(BF16)
    w_lr_t = wt[N_MAIN:N_MAIN + GLA_GATE_RANK]
    w_g_t = (0.5 * wt[N_MAIN + GLA_GATE_RANK:]).astype(BF16)
    w_up_t = jnp.transpose(w_gate_up[0])
    w_pm = jnp.stack([_block_diag([w_pool_mix[0, g], w_pool_mix[0, g + 1]])
                      for g in range(0, POOL_GROUPS, 2)]).astype(BF16)
    w_po = w_pool_out[0].astype(BF16)
    w_go = w_gla_out[0].astype(BF16)
    w_o = (0.5 * w_out[0]).astype(BF16)
    w_1 = w_ff1[0].astype(BF16)
    w_2 = w_ff2[0].astype(BF16)
    n_fin = norm_final.reshape(1, D_MODEL)

    a_meta, st1, w_gate_t = pl.pallas_call(
        _meta_kernel,
        out_shape=(jax.ShapeDtypeStruct((N_META, POOL_WIDTH), F32), jax.ShapeDtypeStruct((GLA_KEY, GLA_DV), F32),
                   jax.ShapeDtypeStruct((GLA_KEY, D_MODEL), BF16)),
        name="meta",
    )(meta_tokens, norm_mix, w_main_t, w_lr_t, w_up_t, b_gate)

    h1 = pl.pallas_call(
        _mixer_kernel,
        grid=(batch, seq // SEQ_TILE),
        in_specs=[
            pl.BlockSpec((None, SEQ_TILE, D_MODEL), lambda b, j: (b, j, 0)),
            _resident((1, D_MODEL)),
            _resident((N_MAIN, D_MODEL)),
            _resident((GLA_KEY, D_MODEL)),
            _resident((2 * D_MODEL, D_MODEL)),
            _resident((POOL_GROUPS // 2, 2 * POOL_GROUP_DIM, 2 * POOL_GROUP_DIM)),
            _resident((1, POOL_WIDTH)),
            _resident((POOL_WIDTH, D_MODEL)),
            _resident((1, GLA_KEY)),
            _resident((1, GLA_VAL)),
            _resident((GLA_VAL, D_MODEL)),
            _resident((D_MODEL, D_MODEL)),
            _resident((N_META, POOL_WIDTH)),
            _resident((GLA_KEY, GLA_DV)),
        ],
        out_specs=pl.BlockSpec((None, SEQ_TILE, D_MODEL), lambda b, j: (b, j, 0)),
        out_shape=jax.ShapeDtypeStruct((batch, seq, D_MODEL), F32),
        scratch_shapes=[pltpu.VMEM((POOL_BASE + SEQ_TILE, POOL_WIDTH), F32)] * 3 + [pltpu.VMEM((GLA_KEY, GLA_DV), F32)],
        compiler_params=pltpu.CompilerParams(dimension_semantics=("arbitrary", "arbitrary"),
                                             vmem_limit_bytes=VMEM_LIMIT_BYTES),
        name="mixer",
    )(x, norm_mix, w_main_t, w_gate_t, w_g_t, w_pm, pool_scale, w_po, b_gate, gla_norm, w_go, w_o, a_meta, st1)

    n_rows = batch * seq
    out = pl.pallas_call(
        _mlp_kernel,
        grid=(n_rows // ROW_TILE,),
        in_specs=[
            pl.BlockSpec((ROW_TILE, D_MODEL), lambda i: (i, 0)),
            _resident((1, D_MODEL)),
            _resident((D_MODEL, D_FF)),
            _resident((D_FF, D_MODEL)),
            _resident((1, D_MODEL)),
        ],
        out_specs=pl.BlockSpec((ROW_TILE, D_MODEL), lambda i: (i, 0)),
        out_shape=jax.ShapeDtypeStruct((n_rows, D_MODEL), F32),
        compiler_params=pltpu.CompilerParams(dimension_semantics=("arbitrary",),
                                             vmem_limit_bytes=VMEM_LIMIT_BYTES),
        name="mlp",
    )(h1.reshape(n_rows, D_MODEL), norm_ffn, w_1, w_2, n_fin)
    return out.reshape(batch, seq, D_MODEL)
```

```python
import jax
import jax.numpy as jnp
import numpy as np
from jax import lax
from jax.experimental import pallas as pl
from jax.experimental.pallas import tpu as pltpu

F32 = jnp.float32
BF16 = jnp.bfloat16

D_MODEL = 1024
N_META = 16
CHUNK = 64
EPS = 1e-6
POOL_WIDTH = 512
POOL_GROUPS = 4
POOL_GROUP_DIM = POOL_WIDTH // POOL_GROUPS
POOL_WINDOWS = (2, 4, 8, 16)
MAX_WINDOW = max(POOL_WINDOWS)
GLA_HEADS = 4
GLA_DK = 64
GLA_DV = 128
GLA_KEY = GLA_HEADS * GLA_DK
GLA_VAL = GLA_HEADS * GLA_DV
GLA_GATE_RANK = 16
GLA_TAU = 16.0
D_FF = 4 * D_MODEL

V7X_VMEM_BYTES = 64 * 1024 * 1024
VMEM_LIMIT_BYTES = V7X_VMEM_BYTES * 3 // 4

OFF_A = 0
OFF_Q = OFF_A + POOL_WIDTH
OFF_K = OFF_Q + GLA_KEY
OFF_V = OFF_K + GLA_KEY
OFF_R = OFF_V + GLA_VAL
N_MAIN = OFF_R + GLA_VAL

SEQ_TILE = 1024
ROW_TILE = 1024
MLP_OUT_ROWS = 256
SUB = 16
N_SUB = CHUNK // SUB
POOL_PAD = 8
POOL_BASE = POOL_PAD + MAX_WINDOW


def _rmsnorm(x, g):
    return x * lax.rsqrt(jnp.mean(x * x, axis=-1, keepdims=True) + EPS) * g


def _twice_sigmoid_of_twice(x):
    return 1.0 + jnp.tanh(x)


def _dot(a, b):
    return jnp.dot(a, b, preferred_element_type=F32)


def _dot_nt(a, b):
    return lax.dot_general(a, b, (((1,), (1,)), ((), ())), preferred_element_type=F32)


def _chunk_cumsum(la):
    c = la.shape[0]
    tri = (lax.broadcasted_iota(jnp.int32, (c, c), 1) <= lax.broadcasted_iota(jnp.int32, (c, c), 0)).astype(BF16)
    return _dot(tri, la.astype(BF16))


def _log_decay(gate_lin, bg_ref):
    return jax.nn.log_sigmoid(gate_lin + bg_ref[...]) * (1.0 / GLA_TAU)


def _state_increment(k, v, b):
    kb_t = (k * jnp.exp(b[-1:, :] - b)).T.astype(BF16)
    vb = v.astype(BF16)
    out = []
    for p in range(GLA_HEADS // 2):
        pair = _dot(kb_t[2 * p * GLA_DK:2 * (p + 1) * GLA_DK, :], vb[:, 2 * p * GLA_DV:2 * (p + 1) * GLA_DV])
        out += [pair[:GLA_DK, :GLA_DV], pair[GLA_DK:, GLA_DV:]]
    return jnp.concatenate(out, axis=0)


def _block_diag(blocks):
    n = len(blocks)
    zero = jnp.zeros(blocks[0].shape, blocks[0].dtype)
    return jnp.concatenate(
        [jnp.concatenate([blocks[i] if j == i else zero for j in range(n)], axis=1) for i in range(n)], axis=0)


def _meta_kernel(meta_ref, nmix_ref, wmain_t_ref, wlr_t_ref, wup_t_ref, bg_ref, a_ref, st_ref, wgate_t_ref):
    w_gate_t = jnp.dot(wup_t_ref[...], wlr_t_ref[...], preferred_element_type=F32,
                       precision=lax.Precision.HIGHEST).astype(BF16)
    wgate_t_ref[...] = w_gate_t
    u = _rmsnorm(meta_ref[...], nmix_ref[...]).astype(BF16)
    z = _dot_nt(u, wmain_t_ref[...])
    a_ref[...] = z[:, OFF_A:OFF_A + POOL_WIDTH]
    la = _log_decay(_dot_nt(u, w_gate_t), bg_ref)
    st_ref[...] = _state_increment(z[:, OFF_K:OFF_K + GLA_KEY], z[:, OFF_V:OFF_V + GLA_VAL], _chunk_cumsum(la))


def _score_operands(q, k, b):
    ends = [b[(i + 1) * SUB - 1:(i + 1) * SUB, :] for i in range(N_SUB)]
    rows = lambda e: jnp.broadcast_to(e, (SUB, GLA_KEY))
    own_end = jnp.concatenate([rows(e) for e in ends], axis=0)
    own_start = jnp.concatenate([jnp.zeros((SUB, GLA_KEY), F32)] + [rows(e) for e in ends[:-1]], axis=0)
    x_end = own_end - b
    x_start = b - own_start
    lhs_lo, lhs_up = [], []
    for g in range(N_SUB - 1):
        cut = (g + 1) * SUB
        lhs_lo.append(q[cut:, :] * jnp.exp(b[cut:, :] - ends[g]))
        lhs_up.append(q[:cut, :] * jnp.exp(ends[g] - b[:cut, :]))
    lhs_lo.append(q * jnp.exp(-x_end))
    lhs_up.append(q * jnp.exp(-x_start))

    def blockdiag_t(kk):
        kt = kk.T.astype(BF16)
        return _block_diag([kt[h * GLA_DK:(h + 1) * GLA_DK, :] for h in range(GLA_HEADS)])

    stack = lambda parts: jnp.concatenate(parts, axis=0).astype(BF16)
    return stack(lhs_lo), blockdiag_t(k * jnp.exp(x_end)), stack(lhs_up), blockdiag_t(k * jnp.exp(x_start))


_LO_OFF = [sum(CHUNK - (h + 1) * SUB for h in range(g)) for g in range(N_SUB - 1)]
_UP_OFF = [sum((h + 1) * SUB for h in range(g)) for g in range(N_SUB - 1)]
_DIAG_OFF = sum((h + 1) * SUB for h in range(N_SUB - 1))


def _assemble_scores(r_lo, r_up, col_blk, lower_tri):
    out = []
    for i in range(N_SUB):
        d = _DIAG_OFF + i * SUB
        acc = jnp.where(lower_tri, r_lo[d:d + SUB, :], r_up[d:d + SUB, :])
        for j in range(N_SUB):
            if j < i:
                off = _LO_OFF[j] + (i - j - 1) * SUB
                acc = jnp.where(col_blk == j, r_lo[off:off + SUB, :], acc)
            elif j > i:
                off = _UP_OFF[j - 1] + i * SUB
                acc = jnp.where(col_blk == j, r_up[off:off + SUB, :], acc)
        out.append(acc)
    return jnp.concatenate(out, axis=0)


def _trailing_means(aext_ref, sa_ref, sb_ref, rows):
    assert POOL_WINDOWS == (2, 4, 8, 16) and POOL_PAD == 8
    end = POOL_BASE + rows
    lane = lambda g: slice(g * POOL_GROUP_DIM, POOL_WIDTH)
    one = lambda g: slice(g * POOL_GROUP_DIM, (g + 1) * POOL_GROUP_DIM)
    sa_ref[POOL_PAD:end, :] = aext_ref[POOL_PAD:end, :] + aext_ref[POOL_PAD - 1:end - 1, :]
    sb_ref[POOL_PAD:end, lane(1)] = sa_ref[POOL_PAD:end, lane(1)] + sa_ref[POOL_PAD - 2:end - 2, lane(1)]
    sa_ref[POOL_PAD:end, lane(2)] = sb_ref[POOL_PAD:end, lane(2)] + sb_ref[POOL_PAD - 4:end - 4, lane(2)]
    sums = [sa_ref[POOL_BASE:end, one(0)], sb_ref[POOL_BASE:end, one(1)], sa_ref[POOL_BASE:end, one(2)],
            sa_ref[POOL_BASE:end, one(3)] + sa_ref[POOL_BASE - 8:end - 8, one(3)]]
    return [(sums[g] * (1.0 / w) - aext_ref[POOL_BASE:end, one(g)]).astype(BF16)
            for g, w in enumerate(POOL_WINDOWS)]


def _mixer_kernel(x_ref, nmix_ref, wmain_t_ref, wgate_t_ref, wg_t_ref, wpm_ref, pscale_ref, wpo_ref, bg_ref, gnorm_ref,
                  wgo_ref, wout_ref, ameta_ref, st1_ref, h_ref, aext_ref, psum_a_ref, psum_b_ref, st_ref):
    assert N_SUB == 4
    rows = x_ref.shape[0]
    n_chunks = rows // CHUNK
    chunk = lambda arr, c: arr[c * CHUNK:(c + 1) * CHUNK, :]

    @pl.when(pl.program_id(1) == 0)
    def _():
        zero_pad = jnp.zeros((POOL_PAD, POOL_WIDTH), F32)
        aext_ref[0:POOL_PAD, :] = zero_pad
        psum_a_ref[0:POOL_PAD, :] = zero_pad
        psum_b_ref[0:POOL_PAD, :] = zero_pad
        aext_ref[POOL_PAD:POOL_BASE, :] = ameta_ref[...]
        st_ref[...] = st1_ref[...]

    x = x_ref[...]
    xg = x * nmix_ref[...]
    u_raw = xg.astype(BF16)
    row_scale = lax.rsqrt(jnp.mean(x * x, axis=-1, keepdims=True) + EPS)
    u = (xg * row_scale).astype(BF16)
    proj = lambda lo, hi: _dot_nt(u, wmain_t_ref[lo:hi, :])

    gate_lin = _dot_nt(u_raw, wgate_t_ref[...]) * row_scale
    zqk = _dot_nt(u_raw, wmain_t_ref[OFF_Q:OFF_V, :]) * row_scale
    la = _log_decay(gate_lin, bg_ref)
    v = proj(OFF_V, OFF_R)
    b = [_chunk_cumsum(chunk(la, c)) for c in range(n_chunks)]
    z_a = proj(OFF_A, OFF_Q)
    q = zqk[:, :GLA_KEY]
    k = zqk[:, GLA_KEY:]

    ci = lax.broadcasted_iota(jnp.int32, (SUB, GLA_HEADS * CHUNK), 1) % CHUNK
    col_blk = ci // SUB
    lower_tri = ci % SUB <= lax.broadcasted_iota(jnp.int32, (SUB, GLA_HEADS * CHUNK), 0)
    operands = [_score_operands(chunk(q, c), chunk(k, c), b[c]) for c in range(n_chunks)]
    half = D_MODEL // 2
    raw, incr = [], []

    def scores_and_increments(c0, c1):
        for c in range(c0, c1):
            lhs_lo, rhs_lo, lhs_up, rhs_up = operands[c]
            raw.append((_dot(lhs_lo, rhs_lo), _dot(lhs_up, rhs_up)))
            incr.append(_state_increment(chunk(k, c), chunk(v, c), b[c]))

    group = n_chunks // 8
    scores_and_increments(0, group)
    r = proj(OFF_R, N_MAIN)
    scores_and_increments(group, 2 * group)
    zg_a0 = _dot_nt(u, wg_t_ref[:half, :])
    scores_and_increments(2 * group, 3 * group)

    aext_ref[POOL_BASE:POOL_BASE + rows, :] = z_a
    pooled = _trailing_means(aext_ref, psum_a_ref, psum_b_ref, rows)
    aext_ref[POOL_PAD:POOL_BASE, :] = aext_ref[rows + POOL_PAD:rows + POOL_BASE, :]

    zg_a1 = _dot_nt(u, wg_t_ref[half:D_MODEL, :])
    scores_and_increments(3 * group, 4 * group)
    pm = jnp.concatenate([_dot(jnp.concatenate(pooled[2 * p:2 * p + 2], axis=1), wpm_ref[p])
                          for p in range(POOL_GROUPS // 2)], axis=1)
    pm = (pm * pscale_ref[...]).astype(BF16)
    scores_and_increments(4 * group, 5 * group)
    y_a0 = _dot(pm, wpo_ref[:, :half])
    scores_and_increments(5 * group, 6 * group)
    y_a1 = _dot(pm, wpo_ref[:, half:])
    scores_and_increments(6 * group, n_chunks)

    st = st_ref[...]
    o_chunks = []

    def outputs(st, c0, c1):
        for c in range(c0, c1):
            a = _assemble_scores(raw[c][0], raw[c][1], col_blk, lower_tri).astype(BF16)
            vc = chunk(v, c).astype(BF16)
            stb = st.astype(BF16)
            qb = (chunk(q, c) * jnp.exp(b[c])).astype(BF16)
            o_pairs = []
            for p in range(0, GLA_HEADS, 2):
                heads = (p, p + 1)
                v_blk = _block_diag([vc[:, h * GLA_DV:(h + 1) * GLA_DV] for h in heads])
                s_blk = _block_diag([stb[h * GLA_DK:(h + 1) * GLA_DK, :] for h in heads])
                o_pairs.append(_dot(a[:, p * CHUNK:(p + 2) * CHUNK], v_blk)
                               + _dot(qb[:, p * GLA_DK:(p + 2) * GLA_DK], s_blk))
            o_chunks.append(jnp.concatenate(o_pairs, axis=1))
            decay = jnp.exp(b[c][CHUNK - 8:CHUNK, :].T[:, 7:8])
            st = st * decay + incr[c]
        return st

    st = outputs(st, 0, n_chunks)
    st_ref[...] = st
    o = jnp.concatenate(o_chunks, axis=0)
    y_a = jnp.concatenate([y_a0, y_a1], axis=1)
    zg_a = jnp.concatenate([zg_a0, zg_a1], axis=1)

    zg_b0 = _dot_nt(u, wg_t_ref[D_MODEL:D_MODEL + half, :])
    gnorm = gnorm_ref[...]
    on = []
    for h in range(GLA_HEADS):
        lanes = slice(h * GLA_DV, (h + 1) * GLA_DV)
        rh = r[:, lanes]
        on.append(_rmsnorm(o[:, lanes], gnorm[:, lanes]) * (rh * _twice_sigmoid_of_twice(rh)))
    m_a = _twice_sigmoid_of_twice(zg_a) * y_a
    y_b = _dot(jnp.concatenate(on, axis=1).astype(BF16), wgo_ref[...])
    zg_b1 = _dot_nt(u, wg_t_ref[D_MODEL + half:, :])

    m = m_a + _twice_sigmoid_of_twice(jnp.concatenate([zg_b0, zg_b1], axis=1)) * y_b
    h_ref[...] = x_ref[...] + _dot(m.astype(BF16), wout_ref[...])


def _mlp_kernel(h_ref, nffn_ref, w1_ref, w2_ref, nfin_ref, o_ref):
    h = h_ref[...]
    c = lax.rsqrt(jnp.mean(h * h, axis=-1, keepdims=True) + EPS)
    hid = jnp.square(jnp.maximum(_dot((h * nffn_ref[...]).astype(BF16), w1_ref[...]), 0.0)).astype(BF16)
    for lo in range(0, h_ref.shape[0], MLP_OUT_ROWS):
        rows = slice(lo, lo + MLP_OUT_ROWS)
        h2 = h_ref[rows, :] + _dot(hid[rows, :], w2_ref[...]) * (c[rows, :] * c[rows, :])
        o_ref[rows, :] = _rmsnorm(h2, nfin_ref[...])


def _resident(shape):
    nd = len(shape)
    return pl.BlockSpec(shape, lambda *_: (0,) * nd, pipeline_mode=pl.Buffered(1))


def kernel(x, meta_tokens, norm_mix, w_in, w_pool_mix, pool_scale, w_pool_out, w_gate_up, b_gate, gla_norm, w_gla_out,
           w_out, norm_ffn, w_ff1, w_ff2, norm_final):
    batch, seq, d_model = x.shape
    assert d_model == D_MODEL and norm_mix.shape[0] == 1 and meta_tokens.shape[0] == N_META
    assert seq % SEQ_TILE == 0 and SEQ_TILE % CHUNK == 0 and (batch * seq) % ROW_TILE == 0

    assert GLA_DK == 64
    wt = jnp.transpose(w_in[0])
    col_scale = np.ones((N_MAIN, 1), np.float32)
    col_scale[OFF_Q:OFF_K] = GLA_DK ** -0.5
    col_scale[OFF_R:N_MAIN] = 0.5
    w_main_t = (wt[:N_MAIN] * col_scale).astype(BF16)
    w_lr_t = wt[N_MAIN:N_MAIN + GLA_GATE_RANK]
    w_g_t = (0.5 * wt[N_MAIN + GLA_GATE_RANK:]).astype(BF16)
    w_up_t = jnp.transpose(w_gate_up[0])
    w_pm = jnp.stack([_block_diag([w_pool_mix[0, g], w_pool_mix[0, g + 1]])
                      for g in range(0, POOL_GROUPS, 2)]).astype(BF16)
    w_po = w_pool_out[0].astype(BF16)
    w_go = w_gla_out[0].astype(BF16)
    w_o = (0.5 * w_out[0]).astype(BF16)
    w_1 = w_ff1[0].astype(BF16)
    w_2 = w_ff2[0].astype(BF16)
    n_fin = norm_final.reshape(1, D_MODEL)

    a_meta, st1, w_gate_t = pl.pallas_call(
        _meta_kernel,
        out_shape=(jax.ShapeDtypeStruct((N_META, POOL_WIDTH), F32), jax.ShapeDtypeStruct((GLA_KEY, GLA_DV), F32),
                   jax.ShapeDtypeStruct((GLA_KEY, D_MODEL), BF16)),
        name="meta",
    )(meta_tokens, norm_mix, w_main_t, w_lr_t, w_up_t, b_gate)

    h1 = pl.pallas_call(
        _mixer_kernel,
        grid=(batch, seq // SEQ_TILE),
        in_specs=[
            pl.BlockSpec((None, SEQ_TILE, D_MODEL), lambda b, j: (b, j, 0)),
            _resident((1, D_MODEL)),
            _resident((N_MAIN, D_MODEL)),
            _resident((GLA_KEY, D_MODEL)),
            _resident((2 * D_MODEL, D_MODEL)),
            _resident((POOL_GROUPS // 2, 2 * POOL_GROUP_DIM, 2 * POOL_GROUP_DIM)),
            _resident((1, POOL_WIDTH)),
            _resident((POOL_WIDTH, D_MODEL)),
            _resident((1, GLA_KEY)),
            _resident((1, GLA_VAL)),
            _resident((GLA_VAL, D_MODEL)),
            _resident((D_MODEL, D_MODEL)),
            _resident((N_META, POOL_WIDTH)),
            _resident((GLA_KEY, GLA_DV)),
        ],
        out_specs=pl.BlockSpec((None, SEQ_TILE, D_MODEL), lambda b, j: (b, j, 0)),
        out_shape=jax.ShapeDtypeStruct((batch, seq, D_MODEL), F32),
        scratch_shapes=[pltpu.VMEM((POOL_BASE + SEQ_TILE, POOL_WIDTH), F32)] * 3 + [pltpu.VMEM((GLA_KEY, GLA_DV), F32)],
        compiler_params=pltpu.CompilerParams(dimension_semantics=("arbitrary", "arbitrary"),
                                             vmem_limit_bytes=VMEM_LIMIT_BYTES),
        name="mixer",
    )(x, norm_mix, w_main_t, w_gate_t, w_g_t, w_pm, pool_scale, w_po, b_gate, gla_norm, w_go, w_o, a_meta, st1)

    n_rows = batch * seq
    out = pl.pallas_call(
        _mlp_kernel,
        grid=(n_rows // ROW_TILE,),
        in_specs=[
            pl.BlockSpec((ROW_TILE, D_MODEL), lambda i: (i, 0)),
            _resident((1, D_MODEL)),
            _resident((D_MODEL, D_FF)),
            _resident((D_FF, D_MODEL)),
            _resident((1, D_MODEL)),
        ],
        out_specs=pl.BlockSpec((ROW_TILE, D_MODEL), lambda i: (i, 0)),
        out_shape=jax.ShapeDtypeStruct((n_rows, D_MODEL), F32),
        compiler_params=pltpu.CompilerParams(dimension_semantics=("arbitrary",),
                                             vmem_limit_bytes=VMEM_LIMIT_BYTES),
        name="mlp",
    )(h1.reshape(n_rows, D_MODEL), norm_ffn, w_1, w_2, n_fin)
    return out.reshape(batch, seq, D_MODEL)
```

```python
import functools

import jax
import jax.numpy as jnp
import numpy as np
from jax import lax
from jax.experimental import pallas as pl
from jax.experimental.pallas import tpu as pltpu

F32 = jnp.float32
BF16 = jnp.bfloat16

D_MODEL = 1024
N_META = 16
CHUNK = 64
EPS = 1e-6
POOL_WIDTH = 512
POOL_GROUPS = 4
POOL_GROUP_DIM = POOL_WIDTH // POOL_GROUPS
POOL_WINDOWS = (2, 4, 8, 16)
MAX_WINDOW = max(POOL_WINDOWS)
GLA_HEADS = 4
GLA_DK = 64
GLA_DV = 128
GLA_KEY = GLA_HEADS * GLA_DK
GLA_VAL = GLA_HEADS * GLA_DV
GLA_GATE_RANK = 16
GLA_TAU = 16.0
D_FF = 4 * D_MODEL

V7X_VMEM_BYTES = 64 * 1024 * 1024
VMEM_LIMIT_BYTES = V7X_VMEM_BYTES * 3 // 4

OFF_A = 0
OFF_Q = OFF_A + POOL_WIDTH
OFF_K = OFF_Q + GLA_KEY
OFF_V = OFF_K + GLA_KEY
OFF_R = OFF_V + GLA_VAL
N_MAIN = OFF_R + GLA_VAL

SEQ_TILE = 1024
ROW_TILE = 1024
MLP_OUT_ROWS = 256
SUB = 16
N_SUB = CHUNK // SUB
POOL_PAD = 8
POOL_BASE = POOL_PAD + MAX_WINDOW


def _rmsnorm(x, g):
    return x * lax.rsqrt(jnp.mean(x * x, axis=-1, keepdims=True) + EPS) * g


def _twice_sigmoid_of_twice(x):
    return 1.0 + jnp.tanh(x)


def _dot(a, b):
    return jnp.dot(a, b, preferred_element_type=F32)


def _dot_nt(a, b):
    return lax.dot_general(a, b, (((1,), (1,)), ((), ())), preferred_element_type=F32)


def _chunk_cumsum(la):
    c = la.shape[0]
    tri = (lax.broadcasted_iota(jnp.int32, (c, c), 1) <= lax.broadcasted_iota(jnp.int32, (c, c), 0)).astype(BF16)
    return _dot(tri, la.astype(BF16))


def _log_decay(gate_lin, bg_ref):
    return jax.nn.log_sigmoid(gate_lin + bg_ref[...]) * (1.0 / GLA_TAU)


def _state_increment(k, v, b):
    kb_t = (k * jnp.exp(b[-1:, :] - b)).T.astype(BF16)
    vb = v.astype(BF16)
    out = []
    for p in range(GLA_HEADS // 2):
        pair = _dot(kb_t[2 * p * GLA_DK:2 * (p + 1) * GLA_DK, :], vb[:, 2 * p * GLA_DV:2 * (p + 1) * GLA_DV])
        out += [pair[:GLA_DK, :GLA_DV], pair[GLA_DK:, GLA_DV:]]
    return jnp.concatenate(out, axis=0)


def _block_diag(blocks):
    n = len(blocks)
    zero = jnp.zeros(blocks[0].shape, blocks[0].dtype)
    return jnp.concatenate(
        [jnp.concatenate([blocks[i] if j == i else zero for j in range(n)], axis=1) for i in range(n)], axis=0)


def _meta_kernel(meta_ref, nmix_ref, wmain_t_ref, wlr_t_ref, wup_t_ref, bg_ref, a_ref, st_ref, wgate_t_ref):
    w_gate_t = jnp.dot(wup_t_ref[...], wlr_t_ref[...], preferred_element_type=F32,
                       precision=lax.Precision.HIGHEST).astype(BF16)
    wgate_t_ref[...] = w_gate_t
    u = _rmsnorm(meta_ref[...], nmix_ref[...]).astype(BF16)
    z = _dot_nt(u, wmain_t_ref[...])
    a_ref[...] = z[:, OFF_A:OFF_A + POOL_WIDTH]
    la = _log_decay(_dot_nt(u, w_gate_t), bg_ref)
    st_ref[...] = _state_increment(z[:, OFF_K:OFF_K + GLA_KEY], z[:, OFF_V:OFF_V + GLA_VAL], _chunk_cumsum(la))


def _score_operands(q, k, b):
    ends = [b[(i + 1) * SUB - 1:(i + 1) * SUB, :] for i in range(N_SUB)]
    rows = lambda e: jnp.broadcast_to(e, (SUB, GLA_KEY))
    own_end = jnp.concatenate([rows(e) for e in ends], axis=0)
    own_start = jnp.concatenate([jnp.zeros((SUB, GLA_KEY), F32)] + [rows(e) for e in ends[:-1]], axis=0)
    x_end = own_end - b
    x_start = b - own_start
    lhs_lo, lhs_up = [], []
    for g in range(N_SUB - 1):
        cut = (g + 1) * SUB
        lhs_lo.append(q[cut:, :] * jnp.exp(b[cut:, :] - ends[g]))
        lhs_up.append(q[:cut, :] * jnp.exp(ends[g] - b[:cut, :]))
    lhs_lo.append(q * jnp.exp(-x_end))
    lhs_up.append(q * jnp.exp(-x_start))

    def blockdiag_t(kk):
        kt = kk.T.astype(BF16)
        return _block_diag([kt[h * GLA_DK:(h + 1) * GLA_DK, :] for h in range(GLA_HEADS)])

    stack = lambda parts: jnp.concatenate(parts, axis=0).astype(BF16)
    return stack(lhs_lo), blockdiag_t(k * jnp.exp(x_end)), stack(lhs_up), blockdiag_t(k * jnp.exp(x_start))


_LO_OFF = [sum(CHUNK - (h + 1) * SUB for h in range(g)) for g in range(N_SUB - 1)]
_UP_OFF = [sum((h + 1) * SUB for h in range(g)) for g in range(N_SUB - 1)]
_DIAG_OFF = sum((h + 1) * SUB for h in range(N_SUB - 1))


def _assemble_scores(r_lo, r_up, col_blk, lower_tri):
    out = []
    for i in range(N_SUB):
        d = _DIAG_OFF + i * SUB
        acc = jnp.where(lower_tri, r_lo[d:d + SUB, :], r_up[d:d + SUB, :])
        for j in range(N_SUB):
            if j < i:
                off = _LO_OFF[j] + (i - j - 1) * SUB
                acc = jnp.where(col_blk == j, r_lo[off:off + SUB, :], acc)
            elif j > i:
                off = _UP_OFF[j - 1] + i * SUB
                acc = jnp.where(col_blk == j, r_up[off:off + SUB, :], acc)
        out.append(acc)
    return jnp.concatenate(out, axis=0)


def _trailing_means(aext_ref, sa_ref, sb_ref, rows):
    assert POOL_WINDOWS == (2, 4, 8, 16) and POOL_PAD == 8
    end = POOL_BASE + rows
    lane = lambda g: slice(g * POOL_GROUP_DIM, POOL_WIDTH)
    one = lambda g: slice(g * POOL_GROUP_DIM, (g + 1) * POOL_GROUP_DIM)
    sa_ref[POOL_PAD:end, :] = aext_ref[POOL_PAD:end, :] + aext_ref[POOL_PAD - 1:end - 1, :]
    sb_ref[POOL_PAD:end, lane(1)] = sa_ref[POOL_PAD:end, lane(1)] + sa_ref[POOL_PAD - 2:end - 2, lane(1)]
    sa_ref[POOL_PAD:end, lane(2)] = sb_ref[POOL_PAD:end, lane(2)] + sb_ref[POOL_PAD - 4:end - 4, lane(2)]
    sums = [sa_ref[POOL_BASE:end, one(0)], sb_ref[POOL_BASE:end, one(1)], sa_ref[POOL_BASE:end, one(2)],
            sa_ref[POOL_BASE:end, one(3)] + sa_ref[POOL_BASE - 8:end - 8, one(3)]]
    return [(sums[g] * (1.0 / w) - aext_ref[POOL_BASE:end, one(g)]).astype(BF16)
            for g, w in enumerate(POOL_WINDOWS)]


def _mixer_kernel(x_ref, nmix_ref, wmain_t_ref, wgate_t_ref, wg_t_ref, wpm_ref, pscale_ref, wpo_ref, bg_ref, gnorm_ref,
                  wgo_ref, wout_ref, ameta_ref, st1_ref, h_ref, aext_ref, psum_a_ref, psum_b_ref, st_ref, *,
                  tiles_per_batch):
    assert N_SUB == 4
    rows = x_ref.shape[0]
    n_chunks = rows // CHUNK
    chunk = lambda arr, c: arr[c * CHUNK:(c + 1) * CHUNK, :]

    @pl.when(pl.program_id(0) % tiles_per_batch == 0)
    def _():
        zero_pad = jnp.zeros((POOL_PAD, POOL_WIDTH), F32)
        aext_ref[0:POOL_PAD, :] = zero_pad
        psum_a_ref[0:POOL_PAD, :] = zero_pad
        psum_b_ref[0:POOL_PAD, :] = zero_pad
        aext_ref[POOL_PAD:POOL_BASE, :] = ameta_ref[...]
        st_ref[...] = st1_ref[...]

    x = x_ref[...]
    xg = x * nmix_ref[...]
    u_raw = xg.astype(BF16)
    row_scale = lax.rsqrt(jnp.mean(x * x, axis=-1, keepdims=True) + EPS)
    u = (xg * row_scale).astype(BF16)
    proj = lambda lo, hi: _dot_nt(u, wmain_t_ref[lo:hi, :])

    gate_lin = _dot_nt(u_raw, wgate_t_ref[...]) * row_scale
    zqk = _dot_nt(u_raw, wmain_t_ref[OFF_Q:OFF_V, :]) * row_scale
    la = _log_decay(gate_lin, bg_ref)
    v = proj(OFF_V, OFF_R)
    b = [_chunk_cumsum(chunk(la, c)) for c in range(n_chunks)]
    z_a = proj(OFF_A, OFF_Q)
    q = zqk[:, :GLA_KEY]
    k = zqk[:, GLA_KEY:]

    ci = lax.broadcasted_iota(jnp.int32, (SUB, GLA_HEADS * CHUNK), 1) % CHUNK
    col_blk = ci // SUB
    lower_tri = ci % SUB <= lax.broadcasted_iota(jnp.int32, (SUB, GLA_HEADS * CHUNK), 0)
    operands = [_score_operands(chunk(q, c), chunk(k, c), b[c]) for c in range(n_chunks)]
    half = D_MODEL // 2
    raw, incr = [], []

    def scores_and_increments(c0, c1):
        for c in range(c0, c1):
            lhs_lo, rhs_lo, lhs_up, rhs_up = operands[c]
            raw.append((_dot(lhs_lo, rhs_lo), _dot(lhs_up, rhs_up)))
            incr.append(_state_increment(chunk(k, c), chunk(v, c), b[c]))

    group = n_chunks // 8
    scores_and_increments(0, group)
    r = proj(OFF_R, N_MAIN)
    scores_and_increments(group, 2 * group)
    zg_a0 = _dot_nt(u, wg_t_ref[:half, :])
    scores_and_increments(2 * group, 3 * group)

    aext_ref[POOL_BASE:POOL_BASE + rows, :] = z_a
    pooled = _trailing_means(aext_ref, psum_a_ref, psum_b_ref, rows)
    aext_ref[POOL_PAD:POOL_BASE, :] = aext_ref[rows + POOL_PAD:rows + POOL_BASE, :]

    zg_a1 = _dot_nt(u, wg_t_ref[half:D_MODEL, :])
    scores_and_increments(3 * group, 4 * group)
    pm = jnp.concatenate([_dot(jnp.concatenate(pooled[2 * p:2 * p + 2], axis=1), wpm_ref[p])
                          for p in range(POOL_GROUPS // 2)], axis=1)
    pm = (pm * pscale_ref[...]).astype(BF16)
    scores_and_increments(4 * group, 5 * group)
    y_a0 = _dot(pm, wpo_ref[:, :half])
    scores_and_increments(5 * group, 6 * group)
    y_a1 = _dot(pm, wpo_ref[:, half:])
    scores_and_increments(6 * group, n_chunks)

    st = st_ref[...]
    o_chunks = []

    def outputs(st, c0, c1):
        for c in range(c0, c1):
            a = _assemble_scores(raw[c][0], raw[c][1], col_blk, lower_tri).astype(BF16)
            vc = chunk(v, c).astype(BF16)
            stb = st.astype(BF16)
            qb = (chunk(q, c) * jnp.exp(b[c])).astype(BF16)
            o_pairs = []
            for p in range(0, GLA_HEADS, 2):
                heads = (p, p + 1)
                v_blk = _block_diag([vc[:, h * GLA_DV:(h + 1) * GLA_DV] for h in heads])
                s_blk = _block_diag([stb[h * GLA_DK:(h + 1) * GLA_DK, :] for h in heads])
                o_pairs.append(_dot(a[:, p * CHUNK:(p + 2) * CHUNK], v_blk)
                               + _dot(qb[:, p * GLA_DK:(p + 2) * GLA_DK], s_blk))
            o_chunks.append(jnp.concatenate(o_pairs, axis=1))
            decay = jnp.exp(b[c][CHUNK - 8:CHUNK, :].T[:, 7:8])
            st = st * decay + incr[c]
        return st

    st = outputs(st, 0, n_chunks)
    st_ref[...] = st
    o = jnp.concatenate(o_chunks, axis=0)
    y_a = jnp.concatenate([y_a0, y_a1], axis=1)
    zg_a = jnp.concatenate([zg_a0, zg_a1], axis=1)

    zg_b0 = _dot_nt(u, wg_t_ref[D_MODEL:D_MODEL + half, :])
    gnorm = gnorm_ref[...]
    on = []
    for h in range(GLA_HEADS):
        lanes = slice(h * GLA_DV, (h + 1) * GLA_DV)
        rh = r[:, lanes]
        on.append(_rmsnorm(o[:, lanes], gnorm[:, lanes]) * (rh * _twice_sigmoid_of_twice(rh)))
    m_a = _twice_sigmoid_of_twice(zg_a) * y_a
    y_b = _dot(jnp.concatenate(on, axis=1).astype(BF16), wgo_ref[...])
    zg_b1 = _dot_nt(u, wg_t_ref[D_MODEL + half:, :])

    m = m_a + _twice_sigmoid_of_twice(jnp.concatenate([zg_b0, zg_b1], axis=1)) * y_b
    h_ref[...] = x_ref[...] + _dot(m.astype(BF16), wout_ref[...])


def _mlp_kernel(h_ref, nffn_ref, w1_ref, w2_ref, nfin_ref, o_ref):
    h = h_ref[...]
    c = lax.rsqrt(jnp.mean(h * h, axis=-1, keepdims=True) + EPS)
    hid = jnp.square(jnp.maximum(_dot((h * nffn_ref[...]).astype(BF16), w1_ref[...]), 0.0)).astype(BF16)
    for lo in range(0, h_ref.shape[0], MLP_OUT_ROWS):
        rows = slice(lo, lo + MLP_OUT_ROWS)
        h2 = h_ref[rows, :] + _dot(hid[rows, :], w2_ref[...]) * (c[rows, :] * c[rows, :])
        o_ref[rows, :] = _rmsnorm(h2, nfin_ref[...])


def _resident(shape):
    nd = len(shape)
    return pl.BlockSpec(shape, lambda *_: (0,) * nd, pipeline_mode=pl.Buffered(1))


def kernel(x, meta_tokens, norm_mix, w_in, w_pool_mix, pool_scale, w_pool_out, w_gate_up, b_gate, gla_norm, w_gla_out,
           w_out, norm_ffn, w_ff1, w_ff2, norm_final):
    batch, seq, d_model = x.shape
    assert d_model == D_MODEL and norm_mix.shape[0] == 1 and meta_tokens.shape[0] == N_META
    assert seq % SEQ_TILE == 0 and SEQ_TILE % CHUNK == 0 and (batch * seq) % ROW_TILE == 0

    assert GLA_DK == 64
    wt = jnp.transpose(w_in[0])
    col_scale = np.ones((N_MAIN, 1), np.float32)
    col_scale[OFF_Q:OFF_K] = GLA_DK ** -0.5
    col_scale[OFF_R:N_MAIN] = 0.5
    w_main_t = (wt[:N_MAIN] * col_scale).astype(BF16)
    w_lr_t = wt[N_MAIN:N_MAIN + GLA_GATE_RANK]
    w_g_t = (0.5 * wt[N_MAIN + GLA_GATE_RANK:]).astype(BF16)
    w_up_t = jnp.transpose(w_gate_up[0])
    w_pm = jnp.stack([_block_diag([w_pool_mix[0, g], w_pool_mix[0, g + 1]])
                      for g in range(0, POOL_GROUPS, 2)]).astype(BF16)
    w_po = w_pool_out[0].astype(BF16)
    w_go = w_gla_out[0].astype(BF16)
    w_o = (0.5 * w_out[0]).astype(BF16)
    w_1 = w_ff1[0].astype(BF16)
    w_2 = w_ff2[0].astype(BF16)
    n_fin = norm_final.reshape(1, D_MODEL)
    n_rows = batch * seq

    a_meta, st1, w_gate_t = pl.pallas_call(
        _meta_kernel,
        out_shape=(jax.ShapeDtypeStruct((N_META, POOL_WIDTH), F32), jax.ShapeDtypeStruct((GLA_KEY, GLA_DV), F32),
                   jax.ShapeDtypeStruct((GLA_KEY, D_MODEL), BF16)),
        name="meta",
    )(meta_tokens, norm_mix, w_main_t, w_lr_t, w_up_t, b_gate)

    h1 = pl.pallas_call(
        functools.partial(_mixer_kernel, tiles_per_batch=seq // SEQ_TILE),
        grid=(n_rows // SEQ_TILE,),
        in_specs=[
            pl.BlockSpec((SEQ_TILE, D_MODEL), lambda i: (i, 0)),
            _resident((1, D_MODEL)),
            _resident((N_MAIN, D_MODEL)),
            _resident((GLA_KEY, D_MODEL)),
            _resident((2 * D_MODEL, D_MODEL)),
            _resident((POOL_GROUPS // 2, 2 * POOL_GROUP_DIM, 2 * POOL_GROUP_DIM)),
            _resident((1, POOL_WIDTH)),
            _resident((POOL_WIDTH, D_MODEL)),
            _resident((1, GLA_KEY)),
            _resident((1, GLA_VAL)),
            _resident((GLA_VAL, D_MODEL)),
            _resident((D_MODEL, D_MODEL)),
            _resident((N_META, POOL_WIDTH)),
            _resident((GLA_KEY, GLA_DV)),
        ],
        out_specs=pl.BlockSpec((SEQ_TILE, D_MODEL), lambda i: (i, 0)),
        out_shape=jax.ShapeDtypeStruct((n_rows, D_MODEL), F32),
        scratch_shapes=[pltpu.VMEM((POOL_BASE + SEQ_TILE, POOL_WIDTH), F32)] * 3 + [pltpu.VMEM((GLA_KEY, GLA_DV), F32)],
        compiler_params=pltpu.CompilerParams(dimension_semantics=("arbitrary",),
                                             vmem_limit_bytes=VMEM_LIMIT_BYTES),
        name="mixer",
    )(x.reshape(n_rows, D_MODEL), norm_mix, w_main_t, w_gate_t, w_g_t, w_pm, pool_scale, w_po, b_gate, gla_norm,
      w_go, w_o, a_meta, st1)

    out = pl.pallas_call(
        _mlp_kernel,
        grid=(n_rows // ROW_TILE,),
        in_specs=[
            pl.BlockSpec((ROW_TILE, D_MODEL), lambda i: (i, 0)),
            _resident((1, D_MODEL)),
            _resident((D_MODEL, D_FF)),
            _resident((D_FF, D_MODEL)),
            _resident((1, D_MODEL)),
        ],
        out_specs=pl.BlockSpec((ROW_TILE, D_MODEL), lambda i: (i, 0)),
        out_shape=jax.ShapeDtypeStruct((n_rows, D_MODEL), F32),
        compiler_params=pltpu.CompilerParams(dimension_semantics=("arbitrary",),
                                             vmem_limit_bytes=VMEM_LIMIT_BYTES),
        name="mlp",
    )(h1, norm_ffn, w_1, w_2, n_fin)
    return out.reshape(batch, seq, D_MODEL)
```

```python
import functools

import jax
import jax.numpy as jnp
import numpy as np
from jax import lax
from jax.experimental import pallas as pl
from jax.experimental.pallas import tpu as pltpu

F32 = jnp.float32
BF16 = jnp.bfloat16

D_MODEL = 1024
N_META = 16
CHUNK = 64
EPS = 1e-6
POOL_WIDTH = 512
POOL_GROUPS = 4
POOL_GROUP_DIM = POOL_WIDTH // POOL_GROUPS
POOL_WINDOWS = (2, 4, 8, 16)
MAX_WINDOW = max(POOL_WINDOWS)
GLA_HEADS = 4
GLA_DK = 64
GLA_DV = 128
GLA_KEY = GLA_HEADS * GLA_DK
GLA_VAL = GLA_HEADS * GLA_DV
GLA_GATE_RANK = 16
GLA_TAU = 16.0
D_FF = 4 * D_MODEL

V7X_VMEM_BYTES = 64 * 1024 * 1024
V7X_F32_SUBLANES = 8
V7X_BF16_SUBLANES = 16
VMEM_LIMIT_BYTES = V7X_VMEM_BYTES * 13 // 16

OFF_A = 0
OFF_Q = OFF_A + POOL_WIDTH
OFF_K = OFF_Q + GLA_KEY
OFF_V = OFF_K + GLA_KEY
OFF_R = OFF_V + GLA_VAL
N_MAIN = OFF_R + GLA_VAL

SEQ_TILE = 1024
ROW_TILE = 1024
MLP_OUT_ROWS = 256
SUB = 16
N_SUB = CHUNK // SUB
POOL_PAD = V7X_F32_SUBLANES
POOL_BASE = POOL_PAD + MAX_WINDOW


def _rmsnorm(x, g):
    return x * lax.rsqrt(jnp.mean(x * x, axis=-1, keepdims=True) + EPS) * g


def _twice_sigmoid_of_twice(x):
    return 1.0 + jnp.tanh(x)


def _dot(a, b):
    return jnp.dot(a, b, preferred_element_type=F32)


def _dot_nt(a, b):
    return lax.dot_general(a, b, (((1,), (1,)), ((), ())), preferred_element_type=F32)


def _chunk_cumsum(la):
    c = la.shape[0]
    tri = (lax.broadcasted_iota(jnp.int32, (c, c), 1) <= lax.broadcasted_iota(jnp.int32, (c, c), 0)).astype(BF16)
    return _dot(tri, la.astype(BF16))


def _log_decay(gate_lin, bg_ref):
    return jax.nn.log_sigmoid(gate_lin + bg_ref[...]) * (1.0 / GLA_TAU)


def _state_increment(k, v, b):
    kb_t = (k * jnp.exp(b[-1:, :] - b)).T.astype(BF16)
    vb = v.astype(BF16)
    out = []
    for p in range(GLA_HEADS // 2):
        pair = _dot(kb_t[2 * p * GLA_DK:2 * (p + 1) * GLA_DK, :], vb[:, 2 * p * GLA_DV:2 * (p + 1) * GLA_DV])
        out += [pair[:GLA_DK, :GLA_DV], pair[GLA_DK:, GLA_DV:]]
    return jnp.concatenate(out, axis=0)


def _block_diag(blocks):
    n = len(blocks)
    zero = jnp.zeros(blocks[0].shape, blocks[0].dtype)
    return jnp.concatenate(
        [jnp.concatenate([blocks[i] if j == i else zero for j in range(n)], axis=1) for i in range(n)], axis=0)


def _meta_kernel(meta_ref, nmix_ref, wmain_t_ref, wlr_t_ref, wup_t_ref, bg_ref, a_ref, st_ref, wgate_t_ref):
    w_gate_t = jnp.dot(wup_t_ref[...], wlr_t_ref[...], preferred_element_type=F32,
                       precision=lax.Precision.HIGHEST).astype(BF16)
    wgate_t_ref[...] = w_gate_t
    u = _rmsnorm(meta_ref[...], nmix_ref[...]).astype(BF16)
    z = _dot_nt(u, wmain_t_ref[...])
    a_ref[...] = z[:, OFF_A:OFF_A + POOL_WIDTH]
    la = _log_decay(_dot_nt(u, w_gate_t), bg_ref)
    st_ref[...] = _state_increment(z[:, OFF_K:OFF_K + GLA_KEY], z[:, OFF_V:OFF_V + GLA_VAL], _chunk_cumsum(la))


def _score_operands(q, k, b):
    ends = [b[(i + 1) * SUB - 1:(i + 1) * SUB, :] for i in range(N_SUB)]
    rows = lambda e: jnp.broadcast_to(e, (SUB, GLA_KEY))
    own_end = jnp.concatenate([rows(e) for e in ends], axis=0)
    own_start = jnp.concatenate([jnp.zeros((SUB, GLA_KEY), F32)] + [rows(e) for e in ends[:-1]], axis=0)
    x_end = own_end - b
    x_start = b - own_start
    lhs_lo, lhs_up = [], []
    for g in range(N_SUB - 1):
        cut = (g + 1) * SUB
        lhs_lo.append(q[cut:, :] * jnp.exp(b[cut:, :] - ends[g]))
        lhs_up.append(q[:cut, :] * jnp.exp(ends[g] - b[:cut, :]))
    lhs_lo.append(q * jnp.exp(-x_end))
    lhs_up.append(q * jnp.exp(-x_start))

    def blockdiag_t(kk):
        kt = kk.T.astype(BF16)
        return _block_diag([kt[h * GLA_DK:(h + 1) * GLA_DK, :] for h in range(GLA_HEADS)])

    stack = lambda parts: jnp.concatenate(parts, axis=0).astype(BF16)
    return stack(lhs_lo), blockdiag_t(k * jnp.exp(x_end)), stack(lhs_up), blockdiag_t(k * jnp.exp(x_start))


_LO_OFF = [sum(CHUNK - (h + 1) * SUB for h in range(g)) for g in range(N_SUB - 1)]
_UP_OFF = [sum((h + 1) * SUB for h in range(g)) for g in range(N_SUB - 1)]
_DIAG_OFF = sum((h + 1) * SUB for h in range(N_SUB - 1))


def _assemble_scores(r_lo, r_up, col_blk, lower_tri):
    out = []
    for i in range(N_SUB):
        d = _DIAG_OFF + i * SUB
        acc = jnp.where(lower_tri, r_lo[d:d + SUB, :], r_up[d:d + SUB, :])
        for j in range(N_SUB):
            if j < i:
                off = _LO_OFF[j] + (i - j - 1) * SUB
                acc = jnp.where(col_blk == j, r_lo[off:off + SUB, :], acc)
            elif j > i:
                off = _UP_OFF[j - 1] + i * SUB
                acc = jnp.where(col_blk == j, r_up[off:off + SUB, :], acc)
        out.append(acc)
    return jnp.concatenate(out, axis=0)


def _trailing_means(aext_ref, sa_ref, sb_ref, rows):
    assert POOL_WINDOWS == (2, 4, 8, 16) and POOL_PAD == 8
    end = POOL_BASE + rows
    lane = lambda g: slice(g * POOL_GROUP_DIM, POOL_WIDTH)
    one = lambda g: slice(g * POOL_GROUP_DIM, (g + 1) * POOL_GROUP_DIM)
    sa_ref[POOL_PAD:end, :] = aext_ref[POOL_PAD:end, :] + aext_ref[POOL_PAD - 1:end - 1, :]
    sb_ref[POOL_PAD:end, lane(1)] = sa_ref[POOL_PAD:end, lane(1)] + sa_ref[POOL_PAD - 2:end - 2, lane(1)]
    sa_ref[POOL_PAD:end, lane(2)] = sb_ref[POOL_PAD:end, lane(2)] + sb_ref[POOL_PAD - 4:end - 4, lane(2)]
    sums = [sa_ref[POOL_BASE:end, one(0)], sb_ref[POOL_BASE:end, one(1)], sa_ref[POOL_BASE:end, one(2)],
            sa_ref[POOL_BASE:end, one(3)] + sa_ref[POOL_BASE - 8:end - 8, one(3)]]
    return [(sums[g] * (1.0 / w) - aext_ref[POOL_BASE:end, one(g)]).astype(BF16)
            for g, w in enumerate(POOL_WINDOWS)]


def _mixer_kernel(x_ref, nmix_ref, wmain_t_ref, wgate_t_ref, wg_t_ref, wpm_ref, pscale_ref, wpo_ref, bg_ref, gnorm_ref,
                  wgo_ref, wout_ref, ameta_ref, st1_ref, w1_f32_ref, w2_f32_ref, h_ref, w1_ref, w2_ref,
                  aext_ref, psum_a_ref, psum_b_ref, st_ref, *, tiles_per_batch):
    assert N_SUB == 4
    rows = x_ref.shape[0]
    n_chunks = rows // CHUNK
    chunk = lambda arr, c: arr[c * CHUNK:(c + 1) * CHUNK, :]

    @pl.when(pl.program_id(0) % tiles_per_batch == 0)
    def _():
        zero_pad = jnp.zeros((POOL_PAD, POOL_WIDTH), F32)
        aext_ref[0:POOL_PAD, :] = zero_pad
        psum_a_ref[0:POOL_PAD, :] = zero_pad
        psum_b_ref[0:POOL_PAD, :] = zero_pad
        aext_ref[POOL_PAD:POOL_BASE, :] = ameta_ref[...]
        st_ref[...] = st1_ref[...]

    x = x_ref[...]
    xg = x * nmix_ref[...]
    u_raw = xg.astype(BF16)
    row_scale = lax.rsqrt(jnp.mean(x * x, axis=-1, keepdims=True) + EPS)
    u = (xg * row_scale).astype(BF16)
    proj = lambda lo, hi: _dot_nt(u, wmain_t_ref[lo:hi, :])

    gate_lin = _dot_nt(u_raw, wgate_t_ref[...]) * row_scale
    zqk = _dot_nt(u_raw, wmain_t_ref[OFF_Q:OFF_V, :]) * row_scale
    la = _log_decay(gate_lin, bg_ref)
    v = proj(OFF_V, OFF_R)
    b = [_chunk_cumsum(chunk(la, c)) for c in range(n_chunks)]
    z_a = proj(OFF_A, OFF_Q)
    q = zqk[:, :GLA_KEY]
    k = zqk[:, GLA_KEY:]

    ci = lax.broadcasted_iota(jnp.int32, (SUB, GLA_HEADS * CHUNK), 1) % CHUNK
    col_blk = ci // SUB
    lower_tri = ci % SUB <= lax.broadcasted_iota(jnp.int32, (SUB, GLA_HEADS * CHUNK), 0)
    operands = [_score_operands(chunk(q, c), chunk(k, c), b[c]) for c in range(n_chunks)]
    half = D_MODEL // 2
    raw, incr = [], []

    def scores_and_increments(c0, c1):
        for c in range(c0, c1):
            lhs_lo, rhs_lo, lhs_up, rhs_up = operands[c]
            raw.append((_dot(lhs_lo, rhs_lo), _dot(lhs_up, rhs_up)))
            incr.append(_state_increment(chunk(k, c), chunk(v, c), b[c]))

    group = n_chunks // 8
    scores_and_increments(0, group)
    r = proj(OFF_R, N_MAIN)
    scores_and_increments(group, 2 * group)
    zg_a0 = _dot_nt(u, wg_t_ref[:half, :])
    scores_and_increments(2 * group, 3 * group)

    aext_ref[POOL_BASE:POOL_BASE + rows, :] = z_a
    pooled = _trailing_means(aext_ref, psum_a_ref, psum_b_ref, rows)
    aext_ref[POOL_PAD:POOL_BASE, :] = aext_ref[rows + POOL_PAD:rows + POOL_BASE, :]

    zg_a1 = _dot_nt(u, wg_t_ref[half:D_MODEL, :])
    scores_and_increments(3 * group, 4 * group)
    pm = jnp.concatenate([_dot(jnp.concatenate(pooled[2 * p:2 * p + 2], axis=1), wpm_ref[p])
                          for p in range(POOL_GROUPS // 2)], axis=1)
    pm = (pm * pscale_ref[...]).astype(BF16)
    scores_and_increments(4 * group, 5 * group)
    y_a0 = _dot(pm, wpo_ref[:, :half])
    scores_and_increments(5 * group, 6 * group)
    y_a1 = _dot(pm, wpo_ref[:, half:])
    scores_and_increments(6 * group, n_chunks)

    st = st_ref[...]
    o_chunks = []

    def outputs(st, c0, c1):
        for c in range(c0, c1):
            a = _assemble_scores(raw[c][0], raw[c][1], col_blk, lower_tri).astype(BF16)
            vc = chunk(v, c).astype(BF16)
            stb = st.astype(BF16)
            qb = (chunk(q, c) * jnp.exp(b[c])).astype(BF16)
            o_pairs = []
            for p in range(0, GLA_HEADS, 2):
                heads = (p, p + 1)
                v_blk = _block_diag([vc[:, h * GLA_DV:(h + 1) * GLA_DV] for h in heads])
                s_blk = _block_diag([stb[h * GLA_DK:(h + 1) * GLA_DK, :] for h in heads])
                o_pairs.append(_dot(a[:, p * CHUNK:(p + 2) * CHUNK], v_blk)
                               + _dot(qb[:, p * GLA_DK:(p + 2) * GLA_DK], s_blk))
            o_chunks.append(jnp.concatenate(o_pairs, axis=1))
            decay = jnp.exp(b[c][CHUNK - 8:CHUNK, :].T[:, 7:8])
            st = st * decay + incr[c]
        return st

    st = outputs(st, 0, n_chunks)
    st_ref[...] = st
    o = jnp.concatenate(o_chunks, axis=0)
    y_a = jnp.concatenate([y_a0, y_a1], axis=1)
    zg_a = jnp.concatenate([zg_a0, zg_a1], axis=1)

    zg_b0 = _dot_nt(u, wg_t_ref[D_MODEL:D_MODEL + half, :])
    gnorm = gnorm_ref[...]
    on = []
    for h in range(GLA_HEADS):
        lanes = slice(h * GLA_DV, (h + 1) * GLA_DV)
        rh = r[:, lanes]
        on.append(_rmsnorm(o[:, lanes], gnorm[:, lanes]) * (rh * _twice_sigmoid_of_twice(rh)))
    m_a = _twice_sigmoid_of_twice(zg_a) * y_a
    y_b = _dot(jnp.concatenate(on, axis=1).astype(BF16), wgo_ref[...])
    zg_b1 = _dot_nt(u, wg_t_ref[D_MODEL + half:, :])

    m = m_a + _twice_sigmoid_of_twice(jnp.concatenate([zg_b0, zg_b1], axis=1)) * y_b
    h_ref[...] = x_ref[...] + _dot(m.astype(BF16), wout_ref[...])

    w1_ref[...] = w1_f32_ref[...].astype(BF16)
    w2_ref[...] = w2_f32_ref[...].astype(BF16)


def _mlp_kernel(h_ref, nffn_ref, w1_ref, w2_ref, nfin_ref, o_ref):
    h = h_ref[...]
    c = lax.rsqrt(jnp.mean(h * h, axis=-1, keepdims=True) + EPS)
    hid = jnp.square(jnp.maximum(_dot((h * nffn_ref[...]).astype(BF16), w1_ref[...]), 0.0)).astype(BF16)
    for lo in range(0, h_ref.shape[0], MLP_OUT_ROWS):
        rows = slice(lo, lo + MLP_OUT_ROWS)
        h2 = h_ref[rows, :] + _dot(hid[rows, :], w2_ref[...]) * (c[rows, :] * c[rows, :])
        o_ref[rows, :] = _rmsnorm(h2, nfin_ref[...])


def _resident(shape):
    nd = len(shape)
    return pl.BlockSpec(shape, lambda *_: (0,) * nd, pipeline_mode=pl.Buffered(1))


def kernel(x, meta_tokens, norm_mix, w_in, w_pool_mix, pool_scale, w_pool_out, w_gate_up, b_gate, gla_norm, w_gla_out,
           w_out, norm_ffn, w_ff1, w_ff2, norm_final):
    batch, seq, d_model = x.shape
    assert d_model == D_MODEL and norm_mix.shape[0] == 1 and meta_tokens.shape[0] == N_META
    assert seq % SEQ_TILE == 0 and SEQ_TILE % CHUNK == 0 and (batch * seq) % ROW_TILE == 0

    assert GLA_DK == 64
    wt = jnp.transpose(w_in[0])
    col_scale = np.ones((N_MAIN, 1), np.float32)
    col_scale[OFF_Q:OFF_K] = GLA_DK ** -0.5
    col_scale[OFF_R:N_MAIN] = 0.5
    w_main_t = (wt[:N_MAIN] * col_scale).astype(BF16)
    w_lr_t = wt[N_MAIN:N_MAIN + GLA_GATE_RANK]
    w_g_t = (0.5 * wt[N_MAIN + GLA_GATE_RANK:]).astype(BF16)
    w_up_t = jnp.transpose(w_gate_up[0])
    w_pm = jnp.stack([_block_diag([w_pool_mix[0, g], w_pool_mix[0, g + 1]])
                      for g in range(0, POOL_GROUPS, 2)]).astype(BF16)
    w_po = w_pool_out[0].astype(BF16)
    w_go = w_gla_out[0].astype(BF16)
    w_o = (0.5 * w_out[0]).astype(BF16)
    n_fin = norm_final.reshape(1, D_MODEL)
    n_rows = batch * seq

    a_meta, st1, w_gate_t = pl.pallas_call(
        _meta_kernel,
        out_shape=(jax.ShapeDtypeStruct((N_META, POOL_WIDTH), F32), jax.ShapeDtypeStruct((GLA_KEY, GLA_DV), F32),
                   jax.ShapeDtypeStruct((GLA_KEY, D_MODEL), BF16)),
        name="meta",
    )(meta_tokens, norm_mix, w_main_t, w_lr_t, w_up_t, b_gate)

    n_steps = n_rows // SEQ_TILE
    assert D_MODEL % (V7X_BF16_SUBLANES * n_steps) == 0 and D_FF % (V7X_BF16_SUBLANES * n_steps) == 0
    h1, w_1, w_2 = pl.pallas_call(
        functools.partial(_mixer_kernel, tiles_per_batch=seq // SEQ_TILE),
        grid=(n_steps,),
        in_specs=[
            pl.BlockSpec((SEQ_TILE, D_MODEL), lambda i: (i, 0)),
            _resident((1, D_MODEL)),
            _resident((N_MAIN, D_MODEL)),
            _resident((GLA_KEY, D_MODEL)),
            _resident((2 * D_MODEL, D_MODEL)),
            _resident((POOL_GROUPS // 2, 2 * POOL_GROUP_DIM, 2 * POOL_GROUP_DIM)),
            _resident((1, POOL_WIDTH)),
            _resident((POOL_WIDTH, D_MODEL)),
            _resident((1, GLA_KEY)),
            _resident((1, GLA_VAL)),
            _resident((GLA_VAL, D_MODEL)),
            _resident((D_MODEL, D_MODEL)),
            _resident((N_META, POOL_WIDTH)),
            _resident((GLA_KEY, GLA_DV)),
            pl.BlockSpec((D_MODEL // n_steps, D_FF), lambda i: (i, 0)),
            pl.BlockSpec((D_FF // n_steps, D_MODEL), lambda i: (i, 0)),
        ],
        out_specs=(pl.BlockSpec((SEQ_TILE, D_MODEL), lambda i: (i, 0)),
                   pl.BlockSpec((D_MODEL // n_steps, D_FF), lambda i: (i, 0)),
                   pl.BlockSpec((D_FF // n_steps, D_MODEL), lambda i: (i, 0))),
        out_shape=(jax.ShapeDtypeStruct((n_rows, D_MODEL), F32), jax.ShapeDtypeStruct((D_MODEL, D_FF), BF16),
                   jax.ShapeDtypeStruct((D_FF, D_MODEL), BF16)),
        scratch_shapes=[pltpu.VMEM((POOL_BASE + SEQ_TILE, POOL_WIDTH), F32)] * 3 + [pltpu.VMEM((GLA_KEY, GLA_DV), F32)],
        compiler_params=pltpu.CompilerParams(dimension_semantics=("arbitrary",),
                                             vmem_limit_bytes=VMEM_LIMIT_BYTES),
        name="mixer",
    )(x.reshape(n_rows, D_MODEL), norm_mix, w_main_t, w_gate_t, w_g_t, w_pm, pool_scale, w_po, b_gate, gla_norm,
      w_go, w_o, a_meta, st1, w_ff1[0], w_ff2[0])

    out = pl.pallas_call(
        _mlp_kernel,
        grid=(n_rows // ROW_TILE,),
        in_specs=[
            pl.BlockSpec((ROW_TILE, D_MODEL), lambda i: (i, 0)),
            _resident((1, D_MODEL)),
            _resident((D_MODEL, D_FF)),
            _resident((D_FF, D_MODEL)),
            _resident((1, D_MODEL)),
        ],
        out_specs=pl.BlockSpec((ROW_TILE, D_MODEL), lambda i: (i, 0)),
        out_shape=jax.ShapeDtypeStruct((n_rows, D_MODEL), F32),
        compiler_params=pltpu.CompilerParams(dimension_semantics=("arbitrary",),
                                             vmem_limit_bytes=VMEM_LIMIT_BYTES),
        name="mlp",
    )(h1, norm_ffn, w_1, w_2, n_fin)
    return out.reshape(batch, seq, D_MODEL)
```

```python
import functools

import jax
import jax.numpy as jnp
import numpy as np
from jax import lax
from jax.experimental import pallas as pl
from jax.experimental.pallas import tpu as pltpu

F32 = jnp.float32
BF16 = jnp.bfloat16

D_MODEL = 1024
N_META = 16
CHUNK = 64
EPS = 1e-6
POOL_WIDTH = 512
POOL_GROUPS = 4
POOL_GROUP_DIM = POOL_WIDTH // POOL_GROUPS
POOL_WINDOWS = (2, 4, 8, 16)
MAX_WINDOW = max(POOL_WINDOWS)
GLA_HEADS = 4
GLA_DK = 64
GLA_DV = 128
GLA_KEY = GLA_HEADS * GLA_DK
GLA_VAL = GLA_HEADS * GLA_DV
GLA_GATE_RANK = 16
GLA_TAU = 16.0
D_FF = 4 * D_MODEL

V7X_VMEM_BYTES = 64 * 1024 * 1024
V7X_F32_SUBLANES = 8
VMEM_LIMIT_BYTES = V7X_VMEM_BYTES * 3 // 4

OFF_A = 0
OFF_Q = OFF_A + POOL_WIDTH
OFF_K = OFF_Q + GLA_KEY
OFF_V = OFF_K + GLA_KEY
OFF_R = OFF_V + GLA_VAL
N_MAIN = OFF_R + GLA_VAL

SEQ_TILE = 1024
ROW_TILE = 1024
MLP_OUT_ROWS = 256
SUB = 16
N_SUB = CHUNK // SUB
POOL_PAD = V7X_F32_SUBLANES
POOL_BASE = POOL_PAD + MAX_WINDOW


def _rmsnorm(x, g):
    return x * lax.rsqrt(jnp.mean(x * x, axis=-1, keepdims=True) + EPS) * g


def _twice_sigmoid_of_twice(x):
    return 1.0 + jnp.tanh(x)


def _dot(a, b):
    return jnp.dot(a, b, preferred_element_type=F32)


def _dot_nt(a, b):
    return lax.dot_general(a, b, (((1,), (1,)), ((), ())), preferred_element_type=F32)


def _chunk_cumsum(la):
    c = la.shape[0]
    tri = (lax.broadcasted_iota(jnp.int32, (c, c), 1) <= lax.broadcasted_iota(jnp.int32, (c, c), 0)).astype(BF16)
    return _dot(tri, la.astype(BF16))


def _log_decay(gate_lin, bg_ref):
    return jax.nn.log_sigmoid(gate_lin + bg_ref[...]) * (1.0 / GLA_TAU)


def _state_increment(k, v, b):
    kb_t = (k * jnp.exp(b[-1:, :] - b)).T.astype(BF16)
    vb = v.astype(BF16)
    out = []
    for p in range(GLA_HEADS // 2):
        pair = _dot(kb_t[2 * p * GLA_DK:2 * (p + 1) * GLA_DK, :], vb[:, 2 * p * GLA_DV:2 * (p + 1) * GLA_DV])
        out += [pair[:GLA_DK, :GLA_DV], pair[GLA_DK:, GLA_DV:]]
    return jnp.concatenate(out, axis=0)


def _block_diag(blocks):
    n = len(blocks)
    zero = jnp.zeros(blocks[0].shape, blocks[0].dtype)
    return jnp.concatenate(
        [jnp.concatenate([blocks[i] if j == i else zero for j in range(n)], axis=1) for i in range(n)], axis=0)


def _meta_kernel(meta_ref, nmix_ref, wmain_t_ref, wlr_t_ref, wup_t_ref, bg_ref, a_ref, st_ref, wgate_t_ref):
    w_gate_t = jnp.dot(wup_t_ref[...], wlr_t_ref[...], preferred_element_type=F32,
                       precision=lax.Precision.HIGHEST).astype(BF16)
    wgate_t_ref[...] = w_gate_t
    u = _rmsnorm(meta_ref[...], nmix_ref[...]).astype(BF16)
    z = _dot_nt(u, wmain_t_ref[...])
    a_ref[...] = z[:, OFF_A:OFF_A + POOL_WIDTH]
    la = _log_decay(_dot_nt(u, w_gate_t), bg_ref)
    st_ref[...] = _state_increment(z[:, OFF_K:OFF_K + GLA_KEY], z[:, OFF_V:OFF_V + GLA_VAL], _chunk_cumsum(la))


def _score_operands(q, k, b):
    ends = [b[(i + 1) * SUB - 1:(i + 1) * SUB, :] for i in range(N_SUB)]
    rows = lambda e: jnp.broadcast_to(e, (SUB, GLA_KEY))
    own_end = jnp.concatenate([rows(e) for e in ends], axis=0)
    own_start = jnp.concatenate([jnp.zeros((SUB, GLA_KEY), F32)] + [rows(e) for e in ends[:-1]], axis=0)
    x_end = own_end - b
    x_start = b - own_start
    lhs_lo, lhs_up = [], []
    for g in range(N_SUB - 1):
        cut = (g + 1) * SUB
        lhs_lo.append(q[cut:, :] * jnp.exp(b[cut:, :] - ends[g]))
        lhs_up.append(q[:cut, :] * jnp.exp(ends[g] - b[:cut, :]))
    lhs_lo.append(q * jnp.exp(-x_end))
    lhs_up.append(q * jnp.exp(-x_start))

    def blockdiag_t(kk):
        kt = kk.T.astype(BF16)
        return _block_diag([kt[h * GLA_DK:(h + 1) * GLA_DK, :] for h in range(GLA_HEADS)])

    stack = lambda parts: jnp.concatenate(parts, axis=0).astype(BF16)
    return stack(lhs_lo), blockdiag_t(k * jnp.exp(x_end)), stack(lhs_up), blockdiag_t(k * jnp.exp(x_start))


_LO_OFF = [sum(CHUNK - (h + 1) * SUB for h in range(g)) for g in range(N_SUB - 1)]
_UP_OFF = [sum((h + 1) * SUB for h in range(g)) for g in range(N_SUB - 1)]
_DIAG_OFF = sum((h + 1) * SUB for h in range(N_SUB - 1))


def _assemble_scores(r_lo, r_up, col_blk, lower_tri):
    out = []
    for i in range(N_SUB):
        d = _DIAG_OFF + i * SUB
        acc = jnp.where(lower_tri, r_lo[d:d + SUB, :], r_up[d:d + SUB, :])
        for j in range(N_SUB):
            if j < i:
                off = _LO_OFF[j] + (i - j - 1) * SUB
                acc = jnp.where(col_blk == j, r_lo[off:off + SUB, :], acc)
            elif j > i:
                off = _UP_OFF[j - 1] + i * SUB
                acc = jnp.where(col_blk == j, r_up[off:off + SUB, :], acc)
        out.append(acc)
    return jnp.concatenate(out, axis=0)


def _trailing_means(aext_ref, sa_ref, sb_ref, rows):
    assert POOL_WINDOWS == (2, 4, 8, 16) and POOL_PAD == 8
    end = POOL_BASE + rows
    lane = lambda g: slice(g * POOL_GROUP_DIM, POOL_WIDTH)
    one = lambda g: slice(g * POOL_GROUP_DIM, (g + 1) * POOL_GROUP_DIM)
    sa_ref[POOL_PAD:end, :] = aext_ref[POOL_PAD:end, :] + aext_ref[POOL_PAD - 1:end - 1, :]
    sb_ref[POOL_PAD:end, lane(1)] = sa_ref[POOL_PAD:end, lane(1)] + sa_ref[POOL_PAD - 2:end - 2, lane(1)]
    sa_ref[POOL_PAD:end, lane(2)] = sb_ref[POOL_PAD:end, lane(2)] + sb_ref[POOL_PAD - 4:end - 4, lane(2)]
    sums = [sa_ref[POOL_BASE:end, one(0)], sb_ref[POOL_BASE:end, one(1)], sa_ref[POOL_BASE:end, one(2)],
            sa_ref[POOL_BASE:end, one(3)] + sa_ref[POOL_BASE - 8:end - 8, one(3)]]
    return [(sums[g] * (1.0 / w) - aext_ref[POOL_BASE:end, one(g)]).astype(BF16)
            for g, w in enumerate(POOL_WINDOWS)]


def _mixer_kernel(x_ref, nmix_ref, wmain_t_ref, wgate_t_ref, wg_t_ref, wpm_ref, pscale_ref, wpo_ref, bg_ref, gnorm_ref,
                  wgo_ref, wout_ref, ameta_ref, st1_ref, h_ref, aext_ref, psum_a_ref, psum_b_ref, st_ref, *,
                  tiles_per_batch):
    assert N_SUB == 4
    rows = x_ref.shape[0]
    n_chunks = rows // CHUNK
    chunk = lambda arr, c: arr[c * CHUNK:(c + 1) * CHUNK, :]

    @pl.when(pl.program_id(0) % tiles_per_batch == 0)
    def _():
        zero_pad = jnp.zeros((POOL_PAD, POOL_WIDTH), F32)
        aext_ref[0:POOL_PAD, :] = zero_pad
        psum_a_ref[0:POOL_PAD, :] = zero_pad
        psum_b_ref[0:POOL_PAD, :] = zero_pad
        aext_ref[POOL_PAD:POOL_BASE, :] = ameta_ref[...]
        st_ref[...] = st1_ref[...]

    x = x_ref[...]
    xg = x * nmix_ref[...]
    u_raw = xg.astype(BF16)
    row_scale = lax.rsqrt(jnp.mean(x * x, axis=-1, keepdims=True) + EPS)
    u = (xg * row_scale).astype(BF16)
    proj = lambda lo, hi: _dot_nt(u, wmain_t_ref[lo:hi, :])

    gate_lin = _dot_nt(u_raw, wgate_t_ref[...]) * row_scale
    zqk = _dot_nt(u_raw, wmain_t_ref[OFF_Q:OFF_V, :]) * row_scale
    la = _log_decay(gate_lin, bg_ref)
    v = proj(OFF_V, OFF_R)
    b = [_chunk_cumsum(chunk(la, c)) for c in range(n_chunks)]
    z_a = proj(OFF_A, OFF_Q)
    q = zqk[:, :GLA_KEY]
    k = zqk[:, GLA_KEY:]

    ci = lax.broadcasted_iota(jnp.int32, (SUB, GLA_HEADS * CHUNK), 1) % CHUNK
    col_blk = ci // SUB
    lower_tri = ci % SUB <= lax.broadcasted_iota(jnp.int32, (SUB, GLA_HEADS * CHUNK), 0)
    operands = [_score_operands(chunk(q, c), chunk(k, c), b[c]) for c in range(n_chunks)]
    half = D_MODEL // 2
    raw, incr = [], []

    def scores_and_increments(c0, c1):
        for c in range(c0, c1):
            lhs_lo, rhs_lo, lhs_up, rhs_up = operands[c]
            incr.append(_state_increment(chunk(k, c), chunk(v, c), b[c]))
            raw.append((_dot(lhs_lo, rhs_lo), _dot(lhs_up, rhs_up)))

    group = n_chunks // 8
    scores_and_increments(0, group)
    r = proj(OFF_R, N_MAIN)
    scores_and_increments(group, 2 * group)
    zg_a0 = _dot_nt(u, wg_t_ref[:half, :])
    scores_and_increments(2 * group, 3 * group)

    aext_ref[POOL_BASE:POOL_BASE + rows, :] = z_a
    pooled = _trailing_means(aext_ref, psum_a_ref, psum_b_ref, rows)
    aext_ref[POOL_PAD:POOL_BASE, :] = aext_ref[rows + POOL_PAD:rows + POOL_BASE, :]

    zg_a1 = _dot_nt(u, wg_t_ref[half:D_MODEL, :])
    scores_and_increments(3 * group, 4 * group)
    pm = jnp.concatenate([_dot(jnp.concatenate(pooled[2 * p:2 * p + 2], axis=1), wpm_ref[p])
                          for p in range(POOL_GROUPS // 2)], axis=1)
    pm = (pm * pscale_ref[...]).astype(BF16)
    scores_and_increments(4 * group, 5 * group)
    y_a0 = _dot(pm, wpo_ref[:, :half])
    scores_and_increments(5 * group, 6 * group)
    y_a1 = _dot(pm, wpo_ref[:, half:])
    scores_and_increments(6 * group, n_chunks)

    st = st_ref[...]
    o_chunks = []

    def outputs(st, c0, c1):
        for c in range(c0, c1):
            a = _assemble_scores(raw[c][0], raw[c][1], col_blk, lower_tri).astype(BF16)
            vc = chunk(v, c).astype(BF16)
            stb = st.astype(BF16)
            qb = (chunk(q, c) * jnp.exp(b[c])).astype(BF16)
            o_pairs = []
            for p in range(0, GLA_HEADS, 2):
                heads = (p, p + 1)
                v_blk = _block_diag([vc[:, h * GLA_DV:(h + 1) * GLA_DV] for h in heads])
                s_blk = _block_diag([stb[h * GLA_DK:(h + 1) * GLA_DK, :] for h in heads])
                o_pairs.append(_dot(a[:, p * CHUNK:(p + 2) * CHUNK], v_blk)
                               + _dot(qb[:, p * GLA_DK:(p + 2) * GLA_DK], s_blk))
            o_chunks.append(jnp.concatenate(o_pairs, axis=1))
            decay = jnp.exp(b[c][CHUNK - 8:CHUNK, :].T[:, 7:8])
            st = st * decay + incr[c]
        return st

    st = outputs(st, 0, n_chunks)
    st_ref[...] = st
    o = jnp.concatenate(o_chunks, axis=0)
    y_a = jnp.concatenate([y_a0, y_a1], axis=1)
    zg_a = jnp.concatenate([zg_a0, zg_a1], axis=1)

    zg_b0 = _dot_nt(u, wg_t_ref[D_MODEL:D_MODEL + half, :])
    gnorm = gnorm_ref[...]
    on = []
    for h in range(GLA_HEADS):
        lanes = slice(h * GLA_DV, (h + 1) * GLA_DV)
        rh = r[:, lanes]
        on.append(_rmsnorm(o[:, lanes], gnorm[:, lanes]) * (rh * _twice_sigmoid_of_twice(rh)))
    m_a = _twice_sigmoid_of_twice(zg_a) * y_a
    y_b = _dot(jnp.concatenate(on, axis=1).astype(BF16), wgo_ref[...])
    zg_b1 = _dot_nt(u, wg_t_ref[D_MODEL + half:, :])

    m = m_a + _twice_sigmoid_of_twice(jnp.concatenate([zg_b0, zg_b1], axis=1)) * y_b
    h_ref[...] = x_ref[...] + _dot(m.astype(BF16), wout_ref[...])


def _mlp_kernel(h_ref, nffn_ref, w1_ref, w2_ref, nfin_ref, o_ref):
    h = h_ref[...]
    c = lax.rsqrt(jnp.mean(h * h, axis=-1, keepdims=True) + EPS)
    hid = jnp.square(jnp.maximum(_dot((h * nffn_ref[...]).astype(BF16), w1_ref[...]), 0.0)).astype(BF16)
    for lo in range(0, h_ref.shape[0], MLP_OUT_ROWS):
        rows = slice(lo, lo + MLP_OUT_ROWS)
        h2 = h_ref[rows, :] + _dot(hid[rows, :], w2_ref[...]) * (c[rows, :] * c[rows, :])
        o_ref[rows, :] = _rmsnorm(h2, nfin_ref[...])


def _resident(shape):
    nd = len(shape)
    return pl.BlockSpec(shape, lambda *_: (0,) * nd, pipeline_mode=pl.Buffered(1))


def kernel(x, meta_tokens, norm_mix, w_in, w_pool_mix, pool_scale, w_pool_out, w_gate_up, b_gate, gla_norm, w_gla_out,
           w_out, norm_ffn, w_ff1, w_ff2, norm_final):
    batch, seq, d_model = x.shape
    assert d_model == D_MODEL and norm_mix.shape[0] == 1 and meta_tokens.shape[0] == N_META
    assert seq % SEQ_TILE == 0 and SEQ_TILE % CHUNK == 0 and (batch * seq) % ROW_TILE == 0

    assert GLA_DK == 64
    wt = jnp.transpose(w_in[0])
    col_scale = np.ones((N_MAIN, 1), np.float32)
    col_scale[OFF_Q:OFF_K] = GLA_DK ** -0.5
    col_scale[OFF_R:N_MAIN] = 0.5
    w_main_t = (wt[:N_MAIN] * col_scale).astype(BF16)
    w_lr_t = wt[N_MAIN:N_MAIN + GLA_GATE_RANK]
    w_g_t = (0.5 * wt[N_MAIN + GLA_GATE_RANK:]).astype(BF16)
    w_up_t = jnp.transpose(w_gate_up[0])
    w_pm = jnp.stack([_block_diag([w_pool_mix[0, g], w_pool_mix[0, g + 1]])
                      for g in range(0, POOL_GROUPS, 2)]).astype(BF16)
    w_po = w_pool_out[0].astype(BF16)
    w_go = w_gla_out[0].astype(BF16)
    w_o = (0.5 * w_out[0]).astype(BF16)
    w_1 = w_ff1[0].astype(BF16)
    w_2 = w_ff2[0].astype(BF16)
    n_fin = norm_final.reshape(1, D_MODEL)
    n_rows = batch * seq

    a_meta, st1, w_gate_t = pl.pallas_call(
        _meta_kernel,
        out_shape=(jax.ShapeDtypeStruct((N_META, POOL_WIDTH), F32), jax.ShapeDtypeStruct((GLA_KEY, GLA_DV), F32),
                   jax.ShapeDtypeStruct((GLA_KEY, D_MODEL), BF16)),
        name="meta",
    )(meta_tokens, norm_mix, w_main_t, w_lr_t, w_up_t, b_gate)

    h1 = pl.pallas_call(
        functools.partial(_mixer_kernel, tiles_per_batch=seq // SEQ_TILE),
        grid=(n_rows // SEQ_TILE,),
        in_specs=[
            pl.BlockSpec((SEQ_TILE, D_MODEL), lambda i: (i, 0)),
            _resident((1, D_MODEL)),
            _resident((N_MAIN, D_MODEL)),
            _resident((GLA_KEY, D_MODEL)),
            _resident((2 * D_MODEL, D_MODEL)),
            _resident((POOL_GROUPS // 2, 2 * POOL_GROUP_DIM, 2 * POOL_GROUP_DIM)),
            _resident((1, POOL_WIDTH)),
            _resident((POOL_WIDTH, D_MODEL)),
            _resident((1, GLA_KEY)),
            _resident((1, GLA_VAL)),
            _resident((GLA_VAL, D_MODEL)),
            _resident((D_MODEL, D_MODEL)),
            _resident((N_META, POOL_WIDTH)),
            _resident((GLA_KEY, GLA_DV)),
        ],
        out_specs=pl.BlockSpec((SEQ_TILE, D_MODEL), lambda i: (i, 0)),
        out_shape=jax.ShapeDtypeStruct((n_rows, D_MODEL), F32),
        scratch_shapes=[pltpu.VMEM((POOL_BASE + SEQ_TILE, POOL_WIDTH), F32)] * 3 + [pltpu.VMEM((GLA_KEY, GLA_DV), F32)],
        compiler_params=pltpu.CompilerParams(dimension_semantics=("arbitrary",),
                                             vmem_limit_bytes=VMEM_LIMIT_BYTES),
        name="mixer",
    )(x.reshape(n_rows, D_MODEL), norm_mix, w_main_t, w_gate_t, w_g_t, w_pm, pool_scale, w_po, b_gate, gla_norm,
      w_go, w_o, a_meta, st1)

    out = pl.pallas_call(
        _mlp_kernel,
        grid=(n_rows // ROW_TILE,),
        in_specs=[
            pl.BlockSpec((ROW_TILE, D_MODEL), lambda i: (i, 0)),
            _resident((1, D_MODEL)),
            _resident((D_MODEL, D_FF)),
            _resident((D_FF, D_MODEL)),
            _resident((1, D_MODEL)),
        ],
        out_specs=pl.BlockSpec((ROW_TILE, D_MODEL), lambda i: (i, 0)),
        out_shape=jax.ShapeDtypeStruct((n_rows, D_MODEL), F32),
        compiler_params=pltpu.CompilerParams(dimension_semantics=("arbitrary",),
                                             vmem_limit_bytes=VMEM_LIMIT_BYTES),
        name="mlp",
    )(h1, norm_ffn, w_1, w_2, n_fin)
    return out.reshape(batch, seq, D_MODEL)
```

```python
import functools

import jax
import jax.numpy as jnp
import numpy as np
from jax import lax
from jax.experimental import pallas as pl
from jax.experimental.pallas import tpu as pltpu

F32 = jnp.float32
BF16 = jnp.bfloat16

D_MODEL = 1024
N_META = 16
CHUNK = 64
EPS = 1e-6
POOL_WIDTH = 512
POOL_GROUPS = 4
POOL_GROUP_DIM = POOL_WIDTH // POOL_GROUPS
POOL_WINDOWS = (2, 4, 8, 16)
MAX_WINDOW = max(POOL_WINDOWS)
GLA_HEADS = 4
GLA_DK = 64
GLA_DV = 128
GLA_KEY = GLA_HEADS * GLA_DK
GLA_VAL = GLA_HEADS * GLA_DV
GLA_GATE_RANK = 16
GLA_TAU = 16.0
D_FF = 4 * D_MODEL

V7X_VMEM_BYTES = 64 * 1024 * 1024
V7X_F32_SUBLANES = 8
V7X_BF16_SUBLANES = 16
VMEM_LIMIT_BYTES = V7X_VMEM_BYTES * 13 // 16

OFF_A = 0
OFF_Q = OFF_A + POOL_WIDTH
OFF_K = OFF_Q + GLA_KEY
OFF_V = OFF_K + GLA_KEY
OFF_R = OFF_V + GLA_VAL
N_MAIN = OFF_R + GLA_VAL

SEQ_TILE = 1024
ROW_TILE = 1024
MLP_OUT_ROWS = 256
SUB = 16
N_SUB = CHUNK // SUB
POOL_PAD = V7X_F32_SUBLANES
POOL_BASE = POOL_PAD + MAX_WINDOW


def _rmsnorm(x, g):
    return x * lax.rsqrt(jnp.mean(x * x, axis=-1, keepdims=True) + EPS) * g


def _twice_sigmoid_of_twice(x):
    return 1.0 + jnp.tanh(x)


def _dot(a, b):
    return jnp.dot(a, b, preferred_element_type=F32)


def _dot_nt(a, b):
    return lax.dot_general(a, b, (((1,), (1,)), ((), ())), preferred_element_type=F32)


def _chunk_cumsum(la):
    c = la.shape[0]
    tri = (lax.broadcasted_iota(jnp.int32, (c, c), 1) <= lax.broadcasted_iota(jnp.int32, (c, c), 0)).astype(BF16)
    return _dot(tri, la.astype(BF16))


def _log_decay(gate_lin, bg_ref):
    return jax.nn.log_sigmoid(gate_lin + bg_ref[...]) * (1.0 / GLA_TAU)


def _state_increment(k, v, b):
    kb_t = (k * jnp.exp(b[-1:, :] - b)).T.astype(BF16)
    vb = v.astype(BF16)
    out = []
    for p in range(GLA_HEADS // 2):
        pair = _dot(kb_t[2 * p * GLA_DK:2 * (p + 1) * GLA_DK, :], vb[:, 2 * p * GLA_DV:2 * (p + 1) * GLA_DV])
        out += [pair[:GLA_DK, :GLA_DV], pair[GLA_DK:, GLA_DV:]]
    return jnp.concatenate(out, axis=0)


def _block_diag(blocks):
    n = len(blocks)
    zero = jnp.zeros(blocks[0].shape, blocks[0].dtype)
    return jnp.concatenate(
        [jnp.concatenate([blocks[i] if j == i else zero for j in range(n)], axis=1) for i in range(n)], axis=0)


def _meta_kernel(meta_ref, nmix_ref, wmain_t_ref, wlr_t_ref, wup_t_ref, bg_ref, a_ref, st_ref, wgate_t_ref):
    w_gate_t = jnp.dot(wup_t_ref[...], wlr_t_ref[...], preferred_element_type=F32,
                       precision=lax.Precision.HIGHEST).astype(BF16)
    wgate_t_ref[...] = w_gate_t
    u = _rmsnorm(meta_ref[...], nmix_ref[...]).astype(BF16)
    z = _dot_nt(u, wmain_t_ref[...])
    a_ref[...] = z[:, OFF_A:OFF_A + POOL_WIDTH]
    la = _log_decay(_dot_nt(u, w_gate_t), bg_ref)
    st_ref[...] = _state_increment(z[:, OFF_K:OFF_K + GLA_KEY], z[:, OFF_V:OFF_V + GLA_VAL], _chunk_cumsum(la))


def _score_operands(q, k, b):
    ends = [b[(i + 1) * SUB - 1:(i + 1) * SUB, :] for i in range(N_SUB)]
    rows = lambda e: jnp.broadcast_to(e, (SUB, GLA_KEY))
    own_end = jnp.concatenate([rows(e) for e in ends], axis=0)
    own_start = jnp.concatenate([jnp.zeros((SUB, GLA_KEY), F32)] + [rows(e) for e in ends[:-1]], axis=0)
    x_end = own_end - b
    x_start = b - own_start
    lhs_lo, lhs_up = [], []
    for g in range(N_SUB - 1):
        cut = (g + 1) * SUB
        lhs_lo.append(q[cut:, :] * jnp.exp(b[cut:, :] - ends[g]))
        lhs_up.append(q[:cut, :] * jnp.exp(ends[g] - b[:cut, :]))
    lhs_lo.append(q * jnp.exp(-x_end))
    lhs_up.append(q * jnp.exp(-x_start))

    def blockdiag_t(kk):
        kt = kk.T.astype(BF16)
        return _block_diag([kt[h * GLA_DK:(h + 1) * GLA_DK, :] for h in range(GLA_HEADS)])

    stack = lambda parts: jnp.concatenate(parts, axis=0).astype(BF16)
    return stack(lhs_lo), blockdiag_t(k * jnp.exp(x_end)), stack(lhs_up), blockdiag_t(k * jnp.exp(x_start))


_LO_OFF = [sum(CHUNK - (h + 1) * SUB for h in range(g)) for g in range(N_SUB - 1)]
_UP_OFF = [sum((h + 1) * SUB for h in range(g)) for g in range(N_SUB - 1)]
_DIAG_OFF = sum((h + 1) * SUB for h in range(N_SUB - 1))


def _assemble_scores(r_lo, r_up, col_blk, lower_tri):
    out = []
    for i in range(N_SUB):
        d = _DIAG_OFF + i * SUB
        acc = jnp.where(lower_tri, r_lo[d:d + SUB, :], r_up[d:d + SUB, :])
        for j in range(N_SUB):
            if j < i:
                off = _LO_OFF[j] + (i - j - 1) * SUB
                acc = jnp.where(col_blk == j, r_lo[off:off + SUB, :], acc)
            elif j > i:
                off = _UP_OFF[j - 1] + i * SUB
                acc = jnp.where(col_blk == j, r_up[off:off + SUB, :], acc)
        out.append(acc)
    return jnp.concatenate(out, axis=0)


def _trailing_means(aext_ref, sa_ref, sb_ref, rows):
    assert POOL_WINDOWS == (2, 4, 8, 16) and POOL_PAD == 8
    end = POOL_BASE + rows
    lane = lambda g: slice(g * POOL_GROUP_DIM, POOL_WIDTH)
    one = lambda g: slice(g * POOL_GROUP_DIM, (g + 1) * POOL_GROUP_DIM)
    sa_ref[POOL_PAD:end, :] = aext_ref[POOL_PAD:end, :] + aext_ref[POOL_PAD - 1:end - 1, :]
    sb_ref[POOL_PAD:end, lane(1)] = sa_ref[POOL_PAD:end, lane(1)] + sa_ref[POOL_PAD - 2:end - 2, lane(1)]
    sa_ref[POOL_PAD:end, lane(2)] = sb_ref[POOL_PAD:end, lane(2)] + sb_ref[POOL_PAD - 4:end - 4, lane(2)]
    sums = [sa_ref[POOL_BASE:end, one(0)], sb_ref[POOL_BASE:end, one(1)], sa_ref[POOL_BASE:end, one(2)],
            sa_ref[POOL_BASE:end, one(3)] + sa_ref[POOL_BASE - 8:end - 8, one(3)]]
    return [(sums[g] * (1.0 / w) - aext_ref[POOL_BASE:end, one(g)]).astype(BF16)
            for g, w in enumerate(POOL_WINDOWS)]


def _mixer_kernel(x_ref, nmix_ref, wmain_t_ref, wgate_t_ref, wg_t_ref, wpm_ref, pscale_ref, wpo_ref, bg_ref, gnorm_ref,
                  wgo_ref, wout_ref, ameta_ref, st1_ref, w1_f32_ref, w2_f32_ref, h_ref, w1_ref, w2_ref,
                  aext_ref, psum_a_ref, psum_b_ref, st_ref, *, tiles_per_batch):
    assert N_SUB == 4
    rows = x_ref.shape[0]
    n_chunks = rows // CHUNK
    chunk = lambda arr, c: arr[c * CHUNK:(c + 1) * CHUNK, :]

    @pl.when(pl.program_id(0) % tiles_per_batch == 0)
    def _():
        zero_pad = jnp.zeros((POOL_PAD, POOL_WIDTH), F32)
        aext_ref[0:POOL_PAD, :] = zero_pad
        psum_a_ref[0:POOL_PAD, :] = zero_pad
        psum_b_ref[0:POOL_PAD, :] = zero_pad
        aext_ref[POOL_PAD:POOL_BASE, :] = ameta_ref[...]
        st_ref[...] = st1_ref[...]

    x = x_ref[...]
    xg = x * nmix_ref[...]
    u_raw = xg.astype(BF16)
    row_scale = lax.rsqrt(jnp.mean(x * x, axis=-1, keepdims=True) + EPS)
    u = (xg * row_scale).astype(BF16)
    proj = lambda lo, hi: _dot_nt(u, wmain_t_ref[lo:hi, :])

    gate_lin = _dot_nt(u_raw, wgate_t_ref[...]) * row_scale
    zqk = _dot_nt(u_raw, wmain_t_ref[OFF_Q:OFF_V, :]) * row_scale
    la = _log_decay(gate_lin, bg_ref)
    v = proj(OFF_V, OFF_R)
    b = [_chunk_cumsum(chunk(la, c)) for c in range(n_chunks)]
    z_a = proj(OFF_A, OFF_Q)
    q = zqk[:, :GLA_KEY]
    k = zqk[:, GLA_KEY:]

    ci = lax.broadcasted_iota(jnp.int32, (SUB, GLA_HEADS * CHUNK), 1) % CHUNK
    col_blk = ci // SUB
    lower_tri = ci % SUB <= lax.broadcasted_iota(jnp.int32, (SUB, GLA_HEADS * CHUNK), 0)
    operands = [_score_operands(chunk(q, c), chunk(k, c), b[c]) for c in range(n_chunks)]
    half = D_MODEL // 2
    raw, incr = [], []

    def scores_and_increments(c0, c1):
        for c in range(c0, c1):
            lhs_lo, rhs_lo, lhs_up, rhs_up = operands[c]
            incr.append(_state_increment(chunk(k, c), chunk(v, c), b[c]))
            raw.append((_dot(lhs_lo, rhs_lo), _dot(lhs_up, rhs_up)))

    group = n_chunks // 8
    scores_and_increments(0, group)
    r = proj(OFF_R, N_MAIN)
    scores_and_increments(group, 2 * group)
    zg_a0 = _dot_nt(u, wg_t_ref[:half, :])
    scores_and_increments(2 * group, 3 * group)

    aext_ref[POOL_BASE:POOL_BASE + rows, :] = z_a
    pooled = _trailing_means(aext_ref, psum_a_ref, psum_b_ref, rows)
    aext_ref[POOL_PAD:POOL_BASE, :] = aext_ref[rows + POOL_PAD:rows + POOL_BASE, :]

    zg_a1 = _dot_nt(u, wg_t_ref[half:D_MODEL, :])
    scores_and_increments(3 * group, 4 * group)
    pm = jnp.concatenate([_dot(jnp.concatenate(pooled[2 * p:2 * p + 2], axis=1), wpm_ref[p])
                          for p in range(POOL_GROUPS // 2)], axis=1)
    pm = (pm * pscale_ref[...]).astype(BF16)
    scores_and_increments(4 * group, 5 * group)
    y_a0 = _dot(pm, wpo_ref[:, :half])
    scores_and_increments(5 * group, 6 * group)
    y_a1 = _dot(pm, wpo_ref[:, half:])
    scores_and_increments(6 * group, n_chunks)

    st = st_ref[...]
    o_chunks = []

    def outputs(st, c0, c1):
        for c in range(c0, c1):
            a = _assemble_scores(raw[c][0], raw[c][1], col_blk, lower_tri).astype(BF16)
            vc = chunk(v, c).astype(BF16)
            stb = st.astype(BF16)
            qb = (chunk(q, c) * jnp.exp(b[c])).astype(BF16)
            o_pairs = []
            for p in range(0, GLA_HEADS, 2):
                heads = (p, p + 1)
                v_blk = _block_diag([vc[:, h * GLA_DV:(h + 1) * GLA_DV] for h in heads])
                s_blk = _block_diag([stb[h * GLA_DK:(h + 1) * GLA_DK, :] for h in heads])
                o_pairs.append(_dot(a[:, p * CHUNK:(p + 2) * CHUNK], v_blk)
                               + _dot(qb[:, p * GLA_DK:(p + 2) * GLA_DK], s_blk))
            o_chunks.append(jnp.concatenate(o_pairs, axis=1))
            decay = jnp.exp(b[c][CHUNK - 8:CHUNK, :].T[:, 7:8])
            st = st * decay + incr[c]
        return st

    st = outputs(st, 0, n_chunks)
    st_ref[...] = st
    o = jnp.concatenate(o_chunks, axis=0)
    y_a = jnp.concatenate([y_a0, y_a1], axis=1)
    zg_a = jnp.concatenate([zg_a0, zg_a1], axis=1)

    zg_b0 = _dot_nt(u, wg_t_ref[D_MODEL:D_MODEL + half, :])
    gnorm = gnorm_ref[...]
    on = []
    for h in range(GLA_HEADS):
        lanes = slice(h * GLA_DV, (h + 1) * GLA_DV)
        rh = r[:, lanes]
        on.append(_rmsnorm(o[:, lanes], gnorm[:, lanes]) * (rh * _twice_sigmoid_of_twice(rh)))
    m_a = _twice_sigmoid_of_twice(zg_a) * y_a
    y_b = _dot(jnp.concatenate(on, axis=1).astype(BF16), wgo_ref[...])
    zg_b1 = _dot_nt(u, wg_t_ref[D_MODEL + half:, :])

    m = m_a + _twice_sigmoid_of_twice(jnp.concatenate([zg_b0, zg_b1], axis=1)) * y_b
    h_ref[...] = x_ref[...] + _dot(m.astype(BF16), wout_ref[...])

    w1_ref[...] = w1_f32_ref[...].astype(BF16)
    w2_ref[...] = w2_f32_ref[...].astype(BF16)


def _mlp_kernel(h_ref, nffn_ref, w1_ref, w2_ref, nfin_ref, o_ref):
    h = h_ref[...]
    c = lax.rsqrt(jnp.mean(h * h, axis=-1, keepdims=True) + EPS)
    hid = jnp.square(jnp.maximum(_dot((h * nffn_ref[...]).astype(BF16), w1_ref[...]), 0.0)).astype(BF16)
    for lo in range(0, h_ref.shape[0], MLP_OUT_ROWS):
        rows = slice(lo, lo + MLP_OUT_ROWS)
        h2 = h_ref[rows, :] + _dot(hid[rows, :], w2_ref[...]) * (c[rows, :] * c[rows, :])
        o_ref[rows, :] = _rmsnorm(h2, nfin_ref[...])


def _resident(shape):
    nd = len(shape)
    return pl.BlockSpec(shape, lambda *_: (0,) * nd, pipeline_mode=pl.Buffered(1))


def kernel(x, meta_tokens, norm_mix, w_in, w_pool_mix, pool_scale, w_pool_out, w_gate_up, b_gate, gla_norm, w_gla_out,
           w_out, norm_ffn, w_ff1, w_ff2, norm_final):
    batch, seq, d_model = x.shape
    assert d_model == D_MODEL and norm_mix.shape[0] == 1 and meta_tokens.shape[0] == N_META
    assert seq % SEQ_TILE == 0 and SEQ_TILE % CHUNK == 0 and (batch * seq) % ROW_TILE == 0

    assert GLA_DK == 64
    wt = jnp.transpose(w_in[0])
    col_scale = np.ones((N_MAIN, 1), np.float32)
    col_scale[OFF_Q:OFF_K] = GLA_DK ** -0.5
    col_scale[OFF_R:N_MAIN] = 0.5
    w_main_t = (wt[:N_MAIN] * col_scale).astype(BF16)
    w_lr_t = wt[N_MAIN:N_MAIN + GLA_GATE_RANK]
    w_g_t = (0.5 * wt[N_MAIN + GLA_GATE_RANK:]).astype(BF16)
    w_up_t = jnp.transpose(w_gate_up[0])
    w_pm = jnp.stack([_block_diag([w_pool_mix[0, g], w_pool_mix[0, g + 1]])
                      for g in range(0, POOL_GROUPS, 2)]).astype(BF16)
    w_po = w_pool_out[0].astype(BF16)
    w_go = w_gla_out[0].astype(BF16)
    w_o = (0.5 * w_out[0]).astype(BF16)
    n_fin = norm_final.reshape(1, D_MODEL)
    n_rows = batch * seq

    a_meta, st1, w_gate_t = pl.pallas_call(
        _meta_kernel,
        out_shape=(jax.ShapeDtypeStruct((N_META, POOL_WIDTH), F32), jax.ShapeDtypeStruct((GLA_KEY, GLA_DV), F32),
                   jax.ShapeDtypeStruct((GLA_KEY, D_MODEL), BF16)),
        name="meta",
    )(meta_tokens, norm_mix, w_main_t, w_lr_t, w_up_t, b_gate)

    n_steps = n_rows // SEQ_TILE
    assert D_MODEL % (V7X_BF16_SUBLANES * n_steps) == 0 and D_FF % (V7X_BF16_SUBLANES * n_steps) == 0
    h1, w_1, w_2 = pl.pallas_call(
        functools.partial(_mixer_kernel, tiles_per_batch=seq // SEQ_TILE),
        grid=(n_steps,),
        in_specs=[
            pl.BlockSpec((SEQ_TILE, D_MODEL), lambda i: (i, 0)),
            _resident((1, D_MODEL)),
            _resident((N_MAIN, D_MODEL)),
            _resident((GLA_KEY, D_MODEL)),
            _resident((2 * D_MODEL, D_MODEL)),
            _resident((POOL_GROUPS // 2, 2 * POOL_GROUP_DIM, 2 * POOL_GROUP_DIM)),
            _resident((1, POOL_WIDTH)),
            _resident((POOL_WIDTH, D_MODEL)),
            _resident((1, GLA_KEY)),
            _resident((1, GLA_VAL)),
            _resident((GLA_VAL, D_MODEL)),
            _resident((D_MODEL, D_MODEL)),
            _resident((N_META, POOL_WIDTH)),
            _resident((GLA_KEY, GLA_DV)),
            pl.BlockSpec((D_MODEL // n_steps, D_FF), lambda i: (i, 0)),
            pl.BlockSpec((D_FF // n_steps, D_MODEL), lambda i: (i, 0)),
        ],
        out_specs=(pl.BlockSpec((SEQ_TILE, D_MODEL), lambda i: (i, 0)),
                   pl.BlockSpec((D_MODEL // n_steps, D_FF), lambda i: (i, 0)),
                   pl.BlockSpec((D_FF // n_steps, D_MODEL), lambda i: (i, 0))),
        out_shape=(jax.ShapeDtypeStruct((n_rows, D_MODEL), F32), jax.ShapeDtypeStruct((D_MODEL, D_FF), BF16),
                   jax.ShapeDtypeStruct((D_FF, D_MODEL), BF16)),
        scratch_shapes=[pltpu.VMEM((POOL_BASE + SEQ_TILE, POOL_WIDTH), F32)] * 3 + [pltpu.VMEM((GLA_KEY, GLA_DV), F32)],
        compiler_params=pltpu.CompilerParams(dimension_semantics=("arbitrary",),
                                             vmem_limit_bytes=VMEM_LIMIT_BYTES),
        name="mixer",
    )(x.reshape(n_rows, D_MODEL), norm_mix, w_main_t, w_gate_t, w_g_t, w_pm, pool_scale, w_po, b_gate, gla_norm,
      w_go, w_o, a_meta, st1, w_ff1[0], w_ff2[0])

    out = pl.pallas_call(
        _mlp_kernel,
        grid=(n_rows // ROW_TILE,),
        in_specs=[
            pl.BlockSpec((ROW_TILE, D_MODEL), lambda i: (i, 0)),
            _resident((1, D_MODEL)),
            _resident((D_MODEL, D_FF)),
            _resident((D_FF, D_MODEL)),
            _resident((1, D_MODEL)),
        ],
        out_specs=pl.BlockSpec((ROW_TILE, D_MODEL), lambda i: (i, 0)),
        out_shape=jax.ShapeDtypeStruct((n_rows, D_MODEL), F32),
        compiler_params=pltpu.CompilerParams(dimension_semantics=("arbitrary",),
                                             vmem_limit_bytes=VMEM_LIMIT_BYTES),
        name="mlp",
    )(h1, norm_ffn, w_1, w_2, n_fin)
    return out.reshape(batch, seq, D_MODEL)
```

```python
import functools

import jax
import jax.numpy as jnp
import numpy as np
from jax import lax
from jax.experimental import pallas as pl
from jax.experimental.pallas import tpu as pltpu

F32 = jnp.float32
BF16 = jnp.bfloat16

D_MODEL = 1024
N_META = 16
CHUNK = 64
EPS = 1e-6
POOL_WIDTH = 512
POOL_GROUPS = 4
POOL_GROUP_DIM = POOL_WIDTH // POOL_GROUPS
POOL_WINDOWS = (2, 4, 8, 16)
MAX_WINDOW = max(POOL_WINDOWS)
GLA_HEADS = 4
GLA_DK = 64
GLA_DV = 128
GLA_KEY = GLA_HEADS * GLA_DK
GLA_VAL = GLA_HEADS * GLA_DV
GLA_GATE_RANK = 16
GLA_TAU = 16.0
D_FF = 4 * D_MODEL

V7X_VMEM_BYTES = 64 * 1024 * 1024
V7X_F32_SUBLANES = 8
V7X_BF16_SUBLANES = 16
VMEM_LIMIT_BYTES = V7X_VMEM_BYTES * 13 // 16

OFF_A = 0
OFF_Q = OFF_A + POOL_WIDTH
OFF_K = OFF_Q + GLA_KEY
OFF_V = OFF_K + GLA_KEY
OFF_R = OFF_V + GLA_VAL
N_MAIN = OFF_R + GLA_VAL

SEQ_TILE = 1024
ROW_TILE = 1024
MLP_OUT_ROWS = 256
SUB = 16
N_SUB = CHUNK // SUB
POOL_PAD = V7X_F32_SUBLANES
POOL_BASE = POOL_PAD + MAX_WINDOW


def _rmsnorm(x, g):
    return x * lax.rsqrt(jnp.mean(x * x, axis=-1, keepdims=True) + EPS) * g


def _twice_sigmoid_of_twice(x):
    return 1.0 + jnp.tanh(x)


def _dot(a, b):
    return jnp.dot(a, b, preferred_element_type=F32)


def _dot_nt(a, b):
    return lax.dot_general(a, b, (((1,), (1,)), ((), ())), preferred_element_type=F32)


def _chunk_cumsum(la):
    c = la.shape[0]
    tri = (lax.broadcasted_iota(jnp.int32, (c, c), 1) <= lax.broadcasted_iota(jnp.int32, (c, c), 0)).astype(BF16)
    return _dot(tri, la.astype(BF16))


def _log_decay(gate_lin, bg_ref):
    return jax.nn.log_sigmoid(gate_lin + bg_ref[...]) * (1.0 / GLA_TAU)


def _state_increment(k, v, b):
    kb_t = (k * jnp.exp(b[-1:, :] - b)).T.astype(BF16)
    vb = v.astype(BF16)
    out = []
    for p in range(GLA_HEADS // 2):
        pair = _dot(kb_t[2 * p * GLA_DK:2 * (p + 1) * GLA_DK, :], vb[:, 2 * p * GLA_DV:2 * (p + 1) * GLA_DV])
        out += [pair[:GLA_DK, :GLA_DV], pair[GLA_DK:, GLA_DV:]]
    return jnp.concatenate(out, axis=0)


def _block_diag(blocks):
    n = len(blocks)
    zero = jnp.zeros(blocks[0].shape, blocks[0].dtype)
    return jnp.concatenate(
        [jnp.concatenate([blocks[i] if j == i else zero for j in range(n)], axis=1) for i in range(n)], axis=0)


def _meta_kernel(meta_ref, nmix_ref, wmain_t_ref, wlr_t_ref, wup_t_ref, bg_ref, a_ref, st_ref, wgate_t_ref):
    w_gate_t = jnp.dot(wup_t_ref[...], wlr_t_ref[...], preferred_element_type=F32,
                       precision=lax.Precision.HIGHEST).astype(BF16)
    wgate_t_ref[...] = w_gate_t
    u = _rmsnorm(meta_ref[...], nmix_ref[...]).astype(BF16)
    z = _dot_nt(u, wmain_t_ref[...])
    a_ref[...] = z[:, OFF_A:OFF_A + POOL_WIDTH]
    la = _log_decay(_dot_nt(u, w_gate_t), bg_ref)
    st_ref[...] = _state_increment(z[:, OFF_K:OFF_K + GLA_KEY], z[:, OFF_V:OFF_V + GLA_VAL], _chunk_cumsum(la))


def _score_operands(q, k, b):
    ends = [b[(i + 1) * SUB - 1:(i + 1) * SUB, :] for i in range(N_SUB)]
    rows = lambda e: jnp.broadcast_to(e, (SUB, GLA_KEY))
    own_end = jnp.concatenate([rows(e) for e in ends], axis=0)
    own_start = jnp.concatenate([jnp.zeros((SUB, GLA_KEY), F32)] + [rows(e) for e in ends[:-1]], axis=0)
    x_end = own_end - b
    x_start = b - own_start
    lhs_lo, lhs_up = [], []
    for g in range(N_SUB - 1):
        cut = (g + 1) * SUB
        lhs_lo.append(q[cut:, :] * jnp.exp(b[cut:, :] - ends[g]))
        lhs_up.append(q[:cut, :] * jnp.exp(ends[g] - b[:cut, :]))
    lhs_lo.append(q * jnp.exp(-x_end))
    lhs_up.append(q * jnp.exp(-x_start))

    def blockdiag_t(kk):
        kt = kk.T.astype(BF16)
        return _block_diag([kt[h * GLA_DK:(h + 1) * GLA_DK, :] for h in range(GLA_HEADS)])

    stack = lambda parts: jnp.concatenate(parts, axis=0).astype(BF16)
    return stack(lhs_lo), blockdiag_t(k * jnp.exp(x_end)), stack(lhs_up), blockdiag_t(k * jnp.exp(x_start))


_LO_OFF = [sum(CHUNK - (h + 1) * SUB for h in range(g)) for g in range(N_SUB - 1)]
_UP_OFF = [sum((h + 1) * SUB for h in range(g)) for g in range(N_SUB - 1)]
_DIAG_OFF = sum((h + 1) * SUB for h in range(N_SUB - 1))


def _assemble_scores(r_lo, r_up, col_blk, lower_tri):
    out = []
    for i in range(N_SUB):
        d = _DIAG_OFF + i * SUB
        acc = jnp.where(lower_tri, r_lo[d:d + SUB, :], r_up[d:d + SUB, :])
        for j in range(N_SUB):
            if j < i:
                off = _LO_OFF[j] + (i - j - 1) * SUB
                acc = jnp.where(col_blk == j, r_lo[off:off + SUB, :], acc)
            elif j > i:
                off = _UP_OFF[j - 1] + i * SUB
                acc = jnp.where(col_blk == j, r_up[off:off + SUB, :], acc)
        out.append(acc)
    return jnp.concatenate(out, axis=0)


def _trailing_means(aext_ref, sa_ref, sb_ref, rows):
    assert POOL_WINDOWS == (2, 4, 8, 16) and POOL_PAD == 8
    end = POOL_BASE + rows
    lane = lambda g: slice(g * POOL_GROUP_DIM, POOL_WIDTH)
    one = lambda g: slice(g * POOL_GROUP_DIM, (g + 1) * POOL_GROUP_DIM)
    sa_ref[POOL_PAD:end, :] = aext_ref[POOL_PAD:end, :] + aext_ref[POOL_PAD - 1:end - 1, :]
    sb_ref[POOL_PAD:end, lane(1)] = sa_ref[POOL_PAD:end, lane(1)] + sa_ref[POOL_PAD - 2:end - 2, lane(1)]
    sa_ref[POOL_PAD:end, lane(2)] = sb_ref[POOL_PAD:end, lane(2)] + sb_ref[POOL_PAD - 4:end - 4, lane(2)]
    sums = [sa_ref[POOL_BASE:end, one(0)], sb_ref[POOL_BASE:end, one(1)], sa_ref[POOL_BASE:end, one(2)],
            sa_ref[POOL_BASE:end, one(3)] + sa_ref[POOL_BASE - 8:end - 8, one(3)]]
    return [(sums[g] * (1.0 / w) - aext_ref[POOL_BASE:end, one(g)]).astype(BF16)
            for g, w in enumerate(POOL_WINDOWS)]


def _mixer_kernel(x_ref, nmix_ref, wmain_t_ref, wgate_t_ref, wg_t_ref, wpm_ref, pscale_ref, wpo_ref, bg_ref, gnorm_ref,
                  wgo_ref, wout_ref, ameta_ref, st1_ref, w1_f32_ref, w2_f32_ref, h_ref, w1_ref, w2_ref,
                  aext_ref, psum_a_ref, psum_b_ref, st_ref, *, tiles_per_batch):
    assert N_SUB == 4
    rows = x_ref.shape[0]
    n_chunks = rows // CHUNK
    chunk = lambda arr, c: arr[c * CHUNK:(c + 1) * CHUNK, :]

    @pl.when(pl.program_id(0) % tiles_per_batch == 0)
    def _():
        zero_pad = jnp.zeros((POOL_PAD, POOL_WIDTH), F32)
        aext_ref[0:POOL_PAD, :] = zero_pad
        psum_a_ref[0:POOL_PAD, :] = zero_pad
        psum_b_ref[0:POOL_PAD, :] = zero_pad
        aext_ref[POOL_PAD:POOL_BASE, :] = ameta_ref[...]
        st_ref[...] = st1_ref[...]

    x = x_ref[...]
    xg = x * nmix_ref[...]
    u_raw = xg.astype(BF16)
    row_scale = lax.rsqrt(jnp.mean(x * x, axis=-1, keepdims=True) + EPS)
    u = (xg * row_scale).astype(BF16)
    proj = lambda lo, hi: _dot_nt(u, wmain_t_ref[lo:hi, :])

    gate_lin = _dot_nt(u_raw, wgate_t_ref[...]) * row_scale
    zqk = _dot_nt(u_raw, wmain_t_ref[OFF_Q:OFF_V, :]) * row_scale
    la = _log_decay(gate_lin, bg_ref)
    v = proj(OFF_V, OFF_R)
    b = [_chunk_cumsum(chunk(la, c)) for c in range(n_chunks)]
    z_a = proj(OFF_A, OFF_Q)
    q = zqk[:, :GLA_KEY]
    k = zqk[:, GLA_KEY:]

    ci = lax.broadcasted_iota(jnp.int32, (SUB, GLA_HEADS * CHUNK), 1) % CHUNK
    col_blk = ci // SUB
    lower_tri = ci % SUB <= lax.broadcasted_iota(jnp.int32, (SUB, GLA_HEADS * CHUNK), 0)
    operands = [_score_operands(chunk(q, c), chunk(k, c), b[c]) for c in range(n_chunks)]
    half = D_MODEL // 2
    raw, incr = [], []

    def scores_and_increments(c0, c1):
        for c in range(c0, c1):
            lhs_lo, rhs_lo, lhs_up, rhs_up = operands[c]
            raw.append((_dot(lhs_lo, rhs_lo), _dot(lhs_up, rhs_up)))
            incr.append(_state_increment(chunk(k, c), chunk(v, c), b[c]))

    group = n_chunks // 8
    scores_and_increments(0, group)
    r = proj(OFF_R, N_MAIN)
    scores_and_increments(group, 2 * group)
    zg_a0 = _dot_nt(u, wg_t_ref[:half, :])
    scores_and_increments(2 * group, 3 * group)

    aext_ref[POOL_BASE:POOL_BASE + rows, :] = z_a
    pooled = _trailing_means(aext_ref, psum_a_ref, psum_b_ref, rows)
    aext_ref[POOL_PAD:POOL_BASE, :] = aext_ref[rows + POOL_PAD:rows + POOL_BASE, :]

    zg_a1 = _dot_nt(u, wg_t_ref[half:D_MODEL, :])
    scores_and_increments(3 * group, 4 * group)
    pm = jnp.concatenate([_dot(jnp.concatenate(pooled[2 * p:2 * p + 2], axis=1), wpm_ref[p])
                          for p in range(POOL_GROUPS // 2)], axis=1)
    pm = (pm * pscale_ref[...]).astype(BF16)
    scores_and_increments(4 * group, 5 * group)
    y_a0 = _dot(pm, wpo_ref[:, :half])
    scores_and_increments(5 * group, 6 * group)
    y_a1 = _dot(pm, wpo_ref[:, half:])
    scores_and_increments(6 * group, n_chunks)

    st = st_ref[...]
    o_chunks = []

    def outputs(st, c0, c1):
        for c in range(c0, c1):
            a = _assemble_scores(raw[c][0], raw[c][1], col_blk, lower_tri).astype(BF16)
            vc = chunk(v, c).astype(BF16)
            stb = st.astype(BF16)
            qb = (chunk(q, c) * jnp.exp(b[c])).astype(BF16)
            o_pairs = []
            for p in range(0, GLA_HEADS, 2):
                heads = (p, p + 1)
                v_blk = _block_diag([vc[:, h * GLA_DV:(h + 1) * GLA_DV] for h in heads])
                s_blk = _block_diag([stb[h * GLA_DK:(h + 1) * GLA_DK, :] for h in heads])
                o_pairs.append(_dot(a[:, p * CHUNK:(p + 2) * CHUNK], v_blk)
                               + _dot(qb[:, p * GLA_DK:(p + 2) * GLA_DK], s_blk))
            o_chunks.append(jnp.concatenate(o_pairs, axis=1))
            decay = jnp.exp(b[c][CHUNK - 8:CHUNK, :].T[:, 7:8])
            st = st * decay + incr[c]
        return st

    st = outputs(st, 0, n_chunks)
    st_ref[...] = st
    o = jnp.concatenate(o_chunks, axis=0)
    y_a = jnp.concatenate([y_a0, y_a1], axis=1)
    zg_a = jnp.concatenate([zg_a0, zg_a1], axis=1)

    zg_b0 = _dot_nt(u, wg_t_ref[D_MODEL:D_MODEL + half, :])
    gnorm = gnorm_ref[...]
    on = []
    for h in range(GLA_HEADS):
        lanes = slice(h * GLA_DV, (h + 1) * GLA_DV)
        rh = r[:, lanes]
        on.append(_rmsnorm(o[:, lanes], gnorm[:, lanes]) * (rh * _twice_sigmoid_of_twice(rh)))
    m_a = _twice_sigmoid_of_twice(zg_a) * y_a
    y_b = _dot(jnp.concatenate(on, axis=1).astype(BF16), wgo_ref[...])
    zg_b1 = _dot_nt(u, wg_t_ref[D_MODEL + half:, :])

    m = m_a + _twice_sigmoid_of_twice(jnp.concatenate([zg_b0, zg_b1], axis=1)) * y_b
    h_ref[...] = x_ref[...] + _dot(m.astype(BF16), wout_ref[...])

    w1_ref[...] = w1_f32_ref[...].astype(BF16)
    w2_ref[...] = w2_f32_ref[...].astype(BF16)


def _mlp_kernel(h_ref, nffn_ref, w1_ref, w2_ref, nfin_ref, o_ref):
    h = h_ref[...]
    c = lax.rsqrt(jnp.mean(h * h, axis=-1, keepdims=True) + EPS)
    hid = jnp.square(jnp.maximum(_dot((h * nffn_ref[...]).astype(BF16), w1_ref[...]), 0.0)).astype(BF16)
    for lo in range(0, h_ref.shape[0], MLP_OUT_ROWS):
        rows = slice(lo, lo + MLP_OUT_ROWS)
        h2 = h_ref[rows, :] + _dot(hid[rows, :], w2_ref[...]) * (c[rows, :] * c[rows, :])
        o_ref[rows, :] = _rmsnorm(h2, nfin_ref[...])


def _resident(shape):
    nd = len(shape)
    return pl.BlockSpec(shape, lambda *_: (0,) * nd, pipeline_mode=pl.Buffered(1))


def kernel(x, meta_tokens, norm_mix, w_in, w_pool_mix, pool_scale, w_pool_out, w_gate_up, b_gate, gla_norm, w_gla_out,
           w_out, norm_ffn, w_ff1, w_ff2, norm_final):
    batch, seq, d_model = x.shape
    assert d_model == D_MODEL and norm_mix.shape[0] == 1 and meta_tokens.shape[0] == N_META
    assert seq % SEQ_TILE == 0 and SEQ_TILE % CHUNK == 0 and (batch * seq) % ROW_TILE == 0

    assert GLA_DK == 64
    wt = jnp.transpose(w_in[0])
    col_scale = np.ones((N_MAIN, 1), np.float32)
    col_scale[OFF_Q:OFF_K] = GLA_DK ** -0.5
    col_scale[OFF_R:N_MAIN] = 0.5
    w_main_t = (wt[:N_MAIN] * col_scale).astype(BF16)
    w_lr_t = wt[N_MAIN:N_MAIN + GLA_GATE_RANK]
    w_g_t = (0.5 * wt[N_MAIN + GLA_GATE_RANK:]).astype(BF16)
    w_up_t = jnp.transpose(w_gate_up[0])
    w_pm = jnp.stack([_block_diag([w_pool_mix[0, g], w_pool_mix[0, g + 1]])
                      for g in range(0, POOL_GROUPS, 2)]).astype(BF16)
    w_po = w_pool_out[0].astype(BF16)
    w_go = w_gla_out[0].astype(BF16)
    w_o = (0.5 * w_out[0]).astype(BF16)
    n_fin = norm_final.reshape(1, D_MODEL)
    n_rows = batch * seq

    a_meta, st1, w_gate_t = pl.pallas_call(
        _meta_kernel,
        out_shape=(jax.ShapeDtypeStruct((N_META, POOL_WIDTH), F32), jax.ShapeDtypeStruct((GLA_KEY, GLA_DV), F32),
                   jax.ShapeDtypeStruct((GLA_KEY, D_MODEL), BF16)),
        name="meta",
    )(meta_tokens, norm_mix, w_main_t, w_lr_t, w_up_t, b_gate)

    n_steps = n_rows // SEQ_TILE
    assert D_MODEL % (V7X_BF16_SUBLANES * n_steps) == 0 and D_FF % (V7X_BF16_SUBLANES * n_steps) == 0
    h1, w_1, w_2 = pl.pallas_call(
        functools.partial(_mixer_kernel, tiles_per_batch=seq // SEQ_TILE),
        grid=(n_steps,),
        in_specs=[
            pl.BlockSpec((SEQ_TILE, D_MODEL), lambda i: (i, 0)),
            _resident((1, D_MODEL)),
            _resident((N_MAIN, D_MODEL)),
            _resident((GLA_KEY, D_MODEL)),
            _resident((2 * D_MODEL, D_MODEL)),
            _resident((POOL_GROUPS // 2, 2 * POOL_GROUP_DIM, 2 * POOL_GROUP_DIM)),
            _resident((1, POOL_WIDTH)),
            _resident((POOL_WIDTH, D_MODEL)),
            _resident((1, GLA_KEY)),
            _resident((1, GLA_VAL)),
            _resident((GLA_VAL, D_MODEL)),
            _resident((D_MODEL, D_MODEL)),
            _resident((N_META, POOL_WIDTH)),
            _resident((GLA_KEY, GLA_DV)),
            pl.BlockSpec((D_MODEL // n_steps, D_FF), lambda i: (i, 0)),
            pl.BlockSpec((D_FF // n_steps, D_MODEL), lambda i: (i, 0)),
        ],
        out_specs=(pl.BlockSpec((SEQ_TILE, D_MODEL), lambda i: (i, 0)),
                   pl.BlockSpec((D_MODEL // n_steps, D_FF), lambda i: (i, 0)),
                   pl.BlockSpec((D_FF // n_steps, D_MODEL), lambda i: (i, 0))),
        out_shape=(jax.ShapeDtypeStruct((n_rows, D_MODEL), F32), jax.ShapeDtypeStruct((D_MODEL, D_FF), BF16),
                   jax.ShapeDtypeStruct((D_FF, D_MODEL), BF16)),
        scratch_shapes=[pltpu.VMEM((POOL_BASE + SEQ_TILE, POOL_WIDTH), F32)] * 3 + [pltpu.VMEM((GLA_KEY, GLA_DV), F32)],
        compiler_params=pltpu.CompilerParams(dimension_semantics=("arbitrary",),
                                             vmem_limit_bytes=VMEM_LIMIT_BYTES),
        name="mixer",
    )(x.reshape(n_rows, D_MODEL), norm_mix, w_main_t, w_gate_t, w_g_t, w_pm, pool_scale, w_po, b_gate, gla_norm,
      w_go, w_o, a_meta, st1, w_ff1[0], w_ff2[0])

    out = pl.pallas_call(
        _mlp_kernel,
        grid=(n_rows // ROW_TILE,),
        in_specs=[
            pl.BlockSpec((ROW_TILE, D_MODEL), lambda i: (i, 0)),
            _resident((1, D_MODEL)),
            _resident((D_MODEL, D_FF)),
            _resident((D_FF, D_MODEL)),
            _resident((1, D_MODEL)),
        ],
        out_specs=pl.BlockSpec((ROW_TILE, D_MODEL), lambda i: (i, 0)),
        out_shape=jax.ShapeDtypeStruct((n_rows, D_MODEL), F32),
        compiler_params=pltpu.CompilerParams(dimension_semantics=("parallel",),
                                             vmem_limit_bytes=VMEM_LIMIT_BYTES),
        name="mlp",
    )(h1, norm_ffn, w_1, w_2, n_fin)
    return out.reshape(batch, seq, D_MODEL)
```

```python
import functools

import jax
import jax.numpy as jnp
import numpy as np
from jax import lax
from jax.experimental import pallas as pl
from jax.experimental.pallas import tpu as pltpu

F32 = jnp.float32
BF16 = jnp.bfloat16

D_MODEL = 1024
N_META = 16
CHUNK = 64
EPS = 1e-6
POOL_WIDTH = 512
POOL_GROUPS = 4
POOL_GROUP_DIM = POOL_WIDTH // POOL_GROUPS
POOL_WINDOWS = (2, 4, 8, 16)
MAX_WINDOW = max(POOL_WINDOWS)
GLA_HEADS = 4
GLA_DK = 64
GLA_DV = 128
GLA_KEY = GLA_HEADS * GLA_DK
GLA_VAL = GLA_HEADS * GLA_DV
GLA_GATE_RANK = 16
GLA_TAU = 16.0
D_FF = 4 * D_MODEL

V7X_VMEM_BYTES = 64 * 1024 * 1024
V7X_F32_SUBLANES = 8
V7X_BF16_SUBLANES = 16
VMEM_LIMIT_BYTES = V7X_VMEM_BYTES * 13 // 16

OFF_A = 0
OFF_Q = OFF_A + POOL_WIDTH
OFF_K = OFF_Q + GLA_KEY
OFF_V = OFF_K + GLA_KEY
OFF_R = OFF_V + GLA_VAL
N_MAIN = OFF_R + GLA_VAL

SEQ_TILE = 1024
ROW_TILE = 1024
MLP_TILES_PER_STEP = 2
MLP_VMEM_LIMIT_BYTES = V7X_VMEM_BYTES * 15 // 16
MLP_OUT_ROWS = 256
SUB = 16
N_SUB = CHUNK // SUB
POOL_PAD = V7X_F32_SUBLANES
POOL_BASE = POOL_PAD + MAX_WINDOW


def _rmsnorm(x, g):
    return x * lax.rsqrt(jnp.mean(x * x, axis=-1, keepdims=True) + EPS) * g


def _twice_sigmoid_of_twice(x):
    return 1.0 + jnp.tanh(x)


def _dot(a, b):
    return jnp.dot(a, b, preferred_element_type=F32)


def _dot_nt(a, b):
    return lax.dot_general(a, b, (((1,), (1,)), ((), ())), preferred_element_type=F32)


def _chunk_cumsum(la):
    c = la.shape[0]
    tri = (lax.broadcasted_iota(jnp.int32, (c, c), 1) <= lax.broadcasted_iota(jnp.int32, (c, c), 0)).astype(BF16)
    return _dot(tri, la.astype(BF16))


def _log_decay(gate_lin, bg_ref):
    return jax.nn.log_sigmoid(gate_lin + bg_ref[...]) * (1.0 / GLA_TAU)


def _state_increment(k, v, b):
    kb_t = (k * jnp.exp(b[-1:, :] - b)).T.astype(BF16)
    vb = v.astype(BF16)
    out = []
    for p in range(GLA_HEADS // 2):
        pair = _dot(kb_t[2 * p * GLA_DK:2 * (p + 1) * GLA_DK, :], vb[:, 2 * p * GLA_DV:2 * (p + 1) * GLA_DV])
        out += [pair[:GLA_DK, :GLA_DV], pair[GLA_DK:, GLA_DV:]]
    return jnp.concatenate(out, axis=0)


def _block_diag(blocks):
    n = len(blocks)
    zero = jnp.zeros(blocks[0].shape, blocks[0].dtype)
    return jnp.concatenate(
        [jnp.concatenate([blocks[i] if j == i else zero for j in range(n)], axis=1) for i in range(n)], axis=0)


def _meta_kernel(meta_ref, nmix_ref, wmain_t_ref, wlr_t_ref, wup_t_ref, bg_ref, a_ref, st_ref, wgate_t_ref):
    w_gate_t = jnp.dot(wup_t_ref[...], wlr_t_ref[...], preferred_element_type=F32,
                       precision=lax.Precision.HIGHEST).astype(BF16)
    wgate_t_ref[...] = w_gate_t
    u = _rmsnorm(meta_ref[...], nmix_ref[...]).astype(BF16)
    z = _dot_nt(u, wmain_t_ref[...])
    a_ref[...] = z[:, OFF_A:OFF_A + POOL_WIDTH]
    la = _log_decay(_dot_nt(u, w_gate_t), bg_ref)
    st_ref[...] = _state_increment(z[:, OFF_K:OFF_K + GLA_KEY], z[:, OFF_V:OFF_V + GLA_VAL], _chunk_cumsum(la))


def _score_operands(q, k, b):
    ends = [b[(i + 1) * SUB - 1:(i + 1) * SUB, :] for i in range(N_SUB)]
    rows = lambda e: jnp.broadcast_to(e, (SUB, GLA_KEY))
    own_end = jnp.concatenate([rows(e) for e in ends], axis=0)
    own_start = jnp.concatenate([jnp.zeros((SUB, GLA_KEY), F32)] + [rows(e) for e in ends[:-1]], axis=0)
    x_end = own_end - b
    x_start = b - own_start
    lhs_lo, lhs_up = [], []
    for g in range(N_SUB - 1):
        cut = (g + 1) * SUB
        lhs_lo.append(q[cut:, :] * jnp.exp(b[cut:, :] - ends[g]))
        lhs_up.append(q[:cut, :] * jnp.exp(ends[g] - b[:cut, :]))
    lhs_lo.append(q * jnp.exp(-x_end))
    lhs_up.append(q * jnp.exp(-x_start))

    def blockdiag_t(kk):
        kt = kk.T.astype(BF16)
        return _block_diag([kt[h * GLA_DK:(h + 1) * GLA_DK, :] for h in range(GLA_HEADS)])

    stack = lambda parts: jnp.concatenate(parts, axis=0).astype(BF16)
    return stack(lhs_lo), blockdiag_t(k * jnp.exp(x_end)), stack(lhs_up), blockdiag_t(k * jnp.exp(x_start))


_LO_OFF = [sum(CHUNK - (h + 1) * SUB for h in range(g)) for g in range(N_SUB - 1)]
_UP_OFF = [sum((h + 1) * SUB for h in range(g)) for g in range(N_SUB - 1)]
_DIAG_OFF = sum((h + 1) * SUB for h in range(N_SUB - 1))


def _assemble_scores(r_lo, r_up, col_blk, lower_tri):
    out = []
    for i in range(N_SUB):
        d = _DIAG_OFF + i * SUB
        acc = jnp.where(lower_tri, r_lo[d:d + SUB, :], r_up[d:d + SUB, :])
        for j in range(N_SUB):
            if j < i:
                off = _LO_OFF[j] + (i - j - 1) * SUB
                acc = jnp.where(col_blk == j, r_lo[off:off + SUB, :], acc)
            elif j > i:
                off = _UP_OFF[j - 1] + i * SUB
                acc = jnp.where(col_blk == j, r_up[off:off + SUB, :], acc)
        out.append(acc)
    return jnp.concatenate(out, axis=0)


def _trailing_means(aext_ref, sa_ref, sb_ref, rows):
    assert POOL_WINDOWS == (2, 4, 8, 16) and POOL_PAD == 8
    end = POOL_BASE + rows
    lane = lambda g: slice(g * POOL_GROUP_DIM, POOL_WIDTH)
    one = lambda g: slice(g * POOL_GROUP_DIM, (g + 1) * POOL_GROUP_DIM)
    sa_ref[POOL_PAD:end, :] = aext_ref[POOL_PAD:end, :] + aext_ref[POOL_PAD - 1:end - 1, :]
    sb_ref[POOL_PAD:end, lane(1)] = sa_ref[POOL_PAD:end, lane(1)] + sa_ref[POOL_PAD - 2:end - 2, lane(1)]
    sa_ref[POOL_PAD:end, lane(2)] = sb_ref[POOL_PAD:end, lane(2)] + sb_ref[POOL_PAD - 4:end - 4, lane(2)]
    sums = [sa_ref[POOL_BASE:end, one(0)], sb_ref[POOL_BASE:end, one(1)], sa_ref[POOL_BASE:end, one(2)],
            sa_ref[POOL_BASE:end, one(3)] + sa_ref[POOL_BASE - 8:end - 8, one(3)]]
    return [(sums[g] * (1.0 / w) - aext_ref[POOL_BASE:end, one(g)]).astype(BF16)
            for g, w in enumerate(POOL_WINDOWS)]


def _mixer_kernel(x_ref, nmix_ref, wmain_t_ref, wgate_t_ref, wg_t_ref, wpm_ref, pscale_ref, wpo_ref, bg_ref, gnorm_ref,
                  wgo_ref, wout_ref, ameta_ref, st1_ref, w1_f32_ref, w2_f32_ref, h_ref, w1_ref, w2_ref,
                  aext_ref, psum_a_ref, psum_b_ref, st_ref, *, tiles_per_batch):
    assert N_SUB == 4
    rows = x_ref.shape[0]
    n_chunks = rows // CHUNK
    chunk = lambda arr, c: arr[c * CHUNK:(c + 1) * CHUNK, :]

    @pl.when(pl.program_id(0) % tiles_per_batch == 0)
    def _():
        zero_pad = jnp.zeros((POOL_PAD, POOL_WIDTH), F32)
        aext_ref[0:POOL_PAD, :] = zero_pad
        psum_a_ref[0:POOL_PAD, :] = zero_pad
        psum_b_ref[0:POOL_PAD, :] = zero_pad
        aext_ref[POOL_PAD:POOL_BASE, :] = ameta_ref[...]
        st_ref[...] = st1_ref[...]

    x = x_ref[...]
    xg = x * nmix_ref[...]
    u_raw = xg.astype(BF16)
    row_scale = lax.rsqrt(jnp.mean(x * x, axis=-1, keepdims=True) + EPS)
    u = (xg * row_scale).astype(BF16)
    proj = lambda lo, hi: _dot_nt(u, wmain_t_ref[lo:hi, :])

    gate_lin = _dot_nt(u_raw, wgate_t_ref[...]) * row_scale
    zqk = _dot_nt(u_raw, wmain_t_ref[OFF_Q:OFF_V, :]) * row_scale
    la = _log_decay(gate_lin, bg_ref)
    v = proj(OFF_V, OFF_R)
    b = [_chunk_cumsum(chunk(la, c)) for c in range(n_chunks)]
    z_a = proj(OFF_A, OFF_Q)
    q = zqk[:, :GLA_KEY]
    k = zqk[:, GLA_KEY:]

    ci = lax.broadcasted_iota(jnp.int32, (SUB, GLA_HEADS * CHUNK), 1) % CHUNK
    col_blk = ci // SUB
    lower_tri = ci % SUB <= lax.broadcasted_iota(jnp.int32, (SUB, GLA_HEADS * CHUNK), 0)
    operands = [_score_operands(chunk(q, c), chunk(k, c), b[c]) for c in range(n_chunks)]
    half = D_MODEL // 2
    raw, incr = [], []

    def scores_and_increments(c0, c1):
        for c in range(c0, c1):
            lhs_lo, rhs_lo, lhs_up, rhs_up = operands[c]
            raw.append((_dot(lhs_lo, rhs_lo), _dot(lhs_up, rhs_up)))
            incr.append(_state_increment(chunk(k, c), chunk(v, c), b[c]))

    group = n_chunks // 8
    scores_and_increments(0, group)
    r = proj(OFF_R, N_MAIN)
    scores_and_increments(group, 2 * group)
    zg_a0 = _dot_nt(u, wg_t_ref[:half, :])
    scores_and_increments(2 * group, 3 * group)

    aext_ref[POOL_BASE:POOL_BASE + rows, :] = z_a
    pooled = _trailing_means(aext_ref, psum_a_ref, psum_b_ref, rows)
    aext_ref[POOL_PAD:POOL_BASE, :] = aext_ref[rows + POOL_PAD:rows + POOL_BASE, :]

    zg_a1 = _dot_nt(u, wg_t_ref[half:D_MODEL, :])
    scores_and_increments(3 * group, 4 * group)
    pm = jnp.concatenate([_dot(jnp.concatenate(pooled[2 * p:2 * p + 2], axis=1), wpm_ref[p])
                          for p in range(POOL_GROUPS // 2)], axis=1)
    pm = (pm * pscale_ref[...]).astype(BF16)
    scores_and_increments(4 * group, 5 * group)
    y_a0 = _dot(pm, wpo_ref[:, :half])
    scores_and_increments(5 * group, 6 * group)
    y_a1 = _dot(pm, wpo_ref[:, half:])
    scores_and_increments(6 * group, n_chunks)

    st = st_ref[...]
    o_chunks = []

    def outputs(st, c0, c1):
        for c in range(c0, c1):
            a = _assemble_scores(raw[c][0], raw[c][1], col_blk, lower_tri).astype(BF16)
            vc = chunk(v, c).astype(BF16)
            stb = st.astype(BF16)
            qb = (chunk(q, c) * jnp.exp(b[c])).astype(BF16)
            o_pairs = []
            for p in range(0, GLA_HEADS, 2):
                heads = (p, p + 1)
                v_blk = _block_diag([vc[:, h * GLA_DV:(h + 1) * GLA_DV] for h in heads])
                s_blk = _block_diag([stb[h * GLA_DK:(h + 1) * GLA_DK, :] for h in heads])
                o_pairs.append(_dot(a[:, p * CHUNK:(p + 2) * CHUNK], v_blk)
                               + _dot(qb[:, p * GLA_DK:(p + 2) * GLA_DK], s_blk))
            o_chunks.append(jnp.concatenate(o_pairs, axis=1))
            decay = jnp.exp(b[c][CHUNK - 8:CHUNK, :].T[:, 7:8])
            st = st * decay + incr[c]
        return st

    st = outputs(st, 0, n_chunks)
    st_ref[...] = st
    o = jnp.concatenate(o_chunks, axis=0)
    y_a = jnp.concatenate([y_a0, y_a1], axis=1)
    zg_a = jnp.concatenate([zg_a0, zg_a1], axis=1)

    zg_b0 = _dot_nt(u, wg_t_ref[D_MODEL:D_MODEL + half, :])
    gnorm = gnorm_ref[...]
    on = []
    for h in range(GLA_HEADS):
        lanes = slice(h * GLA_DV, (h + 1) * GLA_DV)
        rh = r[:, lanes]
        on.append(_rmsnorm(o[:, lanes], gnorm[:, lanes]) * (rh * _twice_sigmoid_of_twice(rh)))
    m_a = _twice_sigmoid_of_twice(zg_a) * y_a
    y_b = _dot(jnp.concatenate(on, axis=1).astype(BF16), wgo_ref[...])
    zg_b1 = _dot_nt(u, wg_t_ref[D_MODEL + half:, :])

    m = m_a + _twice_sigmoid_of_twice(jnp.concatenate([zg_b0, zg_b1], axis=1)) * y_b
    h_ref[...] = x_ref[...] + _dot(m.astype(BF16), wout_ref[...])

    w1_ref[...] = w1_f32_ref[...].astype(BF16)
    w2_ref[...] = w2_f32_ref[...].astype(BF16)


def _mlp_kernel(h_ref, nffn_ref, w1_ref, w2_ref, nfin_ref, o_ref):
    for t0 in range(0, h_ref.shape[0], ROW_TILE):
        h = h_ref[t0:t0 + ROW_TILE, :]
        c = lax.rsqrt(jnp.mean(h * h, axis=-1, keepdims=True) + EPS)
        hid = jnp.square(jnp.maximum(_dot((h * nffn_ref[...]).astype(BF16), w1_ref[...]), 0.0)).astype(BF16)
        for lo in range(0, ROW_TILE, MLP_OUT_ROWS):
            rows = slice(lo, lo + MLP_OUT_ROWS)
            out_rows = slice(t0 + lo, t0 + lo + MLP_OUT_ROWS)
            h2 = h_ref[out_rows, :] + _dot(hid[rows, :], w2_ref[...]) * (c[rows, :] * c[rows, :])
            o_ref[out_rows, :] = _rmsnorm(h2, nfin_ref[...])


def _resident(shape):
    nd = len(shape)
    return pl.BlockSpec(shape, lambda *_: (0,) * nd, pipeline_mode=pl.Buffered(1))


def kernel(x, meta_tokens, norm_mix, w_in, w_pool_mix, pool_scale, w_pool_out, w_gate_up, b_gate, gla_norm, w_gla_out,
           w_out, norm_ffn, w_ff1, w_ff2, norm_final):
    batch, seq, d_model = x.shape
    assert d_model == D_MODEL and norm_mix.shape[0] == 1 and meta_tokens.shape[0] == N_META
    assert seq % SEQ_TILE == 0 and SEQ_TILE % CHUNK == 0 and (batch * seq) % (MLP_TILES_PER_STEP * ROW_TILE) == 0

    assert GLA_DK == 64
    wt = jnp.transpose(w_in[0])
    col_scale = np.ones((N_MAIN, 1), np.float32)
    col_scale[OFF_Q:OFF_K] = GLA_DK ** -0.5
    col_scale[OFF_R:N_MAIN] = 0.5
    w_main_t = (wt[:N_MAIN] * col_scale).astype(BF16)
    w_lr_t = wt[N_MAIN:N_MAIN + GLA_GATE_RANK]
    w_g_t = (0.5 * wt[N_MAIN + GLA_GATE_RANK:]).astype(BF16)
    w_up_t = jnp.transpose(w_gate_up[0])
    w_pm = jnp.stack([_block_diag([w_pool_mix[0, g], w_pool_mix[0, g + 1]])
                      for g in range(0, POOL_GROUPS, 2)]).astype(BF16)
    w_po = w_pool_out[0].astype(BF16)
    w_go = w_gla_out[0].astype(BF16)
    w_o = (0.5 * w_out[0]).astype(BF16)
    n_fin = norm_final.reshape(1, D_MODEL)
    n_rows = batch * seq

    a_meta, st1, w_gate_t = pl.pallas_call(
        _meta_kernel,
        out_shape=(jax.ShapeDtypeStruct((N_META, POOL_WIDTH), F32), jax.ShapeDtypeStruct((GLA_KEY, GLA_DV), F32),
                   jax.ShapeDtypeStruct((GLA_KEY, D_MODEL), BF16)),
        name="meta",
    )(meta_tokens, norm_mix, w_main_t, w_lr_t, w_up_t, b_gate)

    n_steps = n_rows // SEQ_TILE
    assert D_MODEL % (V7X_BF16_SUBLANES * n_steps) == 0 and D_FF % (V7X_BF16_SUBLANES * n_steps) == 0
    h1, w_1, w_2 = pl.pallas_call(
        functools.partial(_mixer_kernel, tiles_per_batch=seq // SEQ_TILE),
        grid=(n_steps,),
        in_specs=[
            pl.BlockSpec((SEQ_TILE, D_MODEL), lambda i: (i, 0)),
            _resident((1, D_MODEL)),
            _resident((N_MAIN, D_MODEL)),
            _resident((GLA_KEY, D_MODEL)),
            _resident((2 * D_MODEL, D_MODEL)),
            _resident((POOL_GROUPS // 2, 2 * POOL_GROUP_DIM, 2 * POOL_GROUP_DIM)),
            _resident((1, POOL_WIDTH)),
            _resident((POOL_WIDTH, D_MODEL)),
            _resident((1, GLA_KEY)),
            _resident((1, GLA_VAL)),
            _resident((GLA_VAL, D_MODEL)),
            _resident((D_MODEL, D_MODEL)),
            _resident((N_META, POOL_WIDTH)),
            _resident((GLA_KEY, GLA_DV)),
            pl.BlockSpec((D_MODEL // n_steps, D_FF), lambda i: (i, 0)),
            pl.BlockSpec((D_FF // n_steps, D_MODEL), lambda i: (i, 0)),
        ],
        out_specs=(pl.BlockSpec((SEQ_TILE, D_MODEL), lambda i: (i, 0)),
                   pl.BlockSpec((D_MODEL // n_steps, D_FF), lambda i: (i, 0)),
                   pl.BlockSpec((D_FF // n_steps, D_MODEL), lambda i: (i, 0))),
        out_shape=(jax.ShapeDtypeStruct((n_rows, D_MODEL), F32), jax.ShapeDtypeStruct((D_MODEL, D_FF), BF16),
                   jax.ShapeDtypeStruct((D_FF, D_MODEL), BF16)),
        scratch_shapes=[pltpu.VMEM((POOL_BASE + SEQ_TILE, POOL_WIDTH), F32)] * 3 + [pltpu.VMEM((GLA_KEY, GLA_DV), F32)],
        compiler_params=pltpu.CompilerParams(dimension_semantics=("arbitrary",),
                                             vmem_limit_bytes=VMEM_LIMIT_BYTES),
        name="mixer",
    )(x.reshape(n_rows, D_MODEL), norm_mix, w_main_t, w_gate_t, w_g_t, w_pm, pool_scale, w_po, b_gate, gla_norm,
      w_go, w_o, a_meta, st1, w_ff1[0], w_ff2[0])

    mlp_rows = MLP_TILES_PER_STEP * ROW_TILE
    out = pl.pallas_call(
        _mlp_kernel,
        grid=(n_rows // mlp_rows,),
        in_specs=[
            pl.BlockSpec((mlp_rows, D_MODEL), lambda i: (i, 0)),
            _resident((1, D_MODEL)),
            _resident((D_MODEL, D_FF)),
            _resident((D_FF, D_MODEL)),
            _resident((1, D_MODEL)),
        ],
        out_specs=pl.BlockSpec((mlp_rows, D_MODEL), lambda i: (i, 0)),
        out_shape=jax.ShapeDtypeStruct((n_rows, D_MODEL), F32),
        compiler_params=pltpu.CompilerParams(dimension_semantics=("arbitrary",),
                                             vmem_limit_bytes=MLP_VMEM_LIMIT_BYTES),
        name="mlp",
    )(h1, norm_ffn, w_1, w_2, n_fin)
    return out.reshape(batch, seq, D_MODEL)
```
